```python
import jax, jax.numpy as jnp
from jax import lax
import numpy as np

D_MODEL = 1024
BATCH = 4
SEQ = 8192
DEPTH = 4

N_HEADS = 8
N_KV_HEADS = 2
HEAD_DIM = 64
ATTN_WIDTH = N_HEADS * HEAD_DIM
KV_WIDTH = N_KV_HEADS * HEAD_DIM
WINDOW = 128
BLOCK = 128
ROPE_DIM = HEAD_DIM // 4
ROPE_THETA = 500000.0
CONV_WIDTH = D_MODEL - ATTN_WIDTH
CONV_K = 3
MIX_WIDTH = ATTN_WIDTH + CONV_WIDTH
IN_WIDTH = ATTN_WIDTH + 2 * KV_WIDTH + 3 * CONV_WIDTH
SPLITS = (ATTN_WIDTH, ATTN_WIDTH + KV_WIDTH, ATTN_WIDTH + 2 * KV_WIDTH,
          ATTN_WIDTH + 2 * KV_WIDTH + CONV_WIDTH, ATTN_WIDTH + 2 * KV_WIDTH + 2 * CONV_WIDTH)
N_EXPERT_GROUPS = 4
EXPERTS_PER_GROUP = 8
N_EXPERTS = N_EXPERT_GROUPS * EXPERTS_PER_GROUP
TOP_K = 2
D_EXPERT = 512
MOE_BLOCK = 256
EPS = 1e-6
NEG_INF = -1e30

kernel_name = "hymba_style_hybrid_encoder"


def rmsnorm(x, g):
    xf = x.astype(jnp.float32)
    y = xf * lax.rsqrt(jnp.mean(xf * xf, axis=-1, keepdims=True) + EPS)
    return (y * g.astype(jnp.float32)).astype(x.dtype)


def rope_tables(seq):
    pos = jnp.arange(seq, dtype=jnp.float32)
    inv_freq = ROPE_THETA ** (-jnp.arange(0, ROPE_DIM, 2, dtype=jnp.float32) / ROPE_DIM)
    ang = pos[:, None] * inv_freq[None, :]
    return jnp.cos(ang)[:, None, :], jnp.sin(ang)[:, None, :]


def partial_rope(x, cos, sin):
    half = ROPE_DIM // 2
    xf = x.astype(jnp.float32)
    x1, x2, rest = xf[..., :half], xf[..., half:ROPE_DIM], xf[..., ROPE_DIM:]
    out = jnp.concatenate([x1 * cos - x2 * sin, x2 * cos + x1 * sin, rest], axis=-1)
    return out.astype(x.dtype)


def windowed_gqa_with_sink(q, k, v, sink):
    b, s = q.shape[0], q.shape[1]
    nb = s // BLOCK
    grp = N_HEADS // N_KV_HEADS
    qb = q.reshape(b, nb, BLOCK, N_KV_HEADS, grp, HEAD_DIM)

    def band(t):
        tp = jnp.pad(t, ((0, 0), (BLOCK, BLOCK), (0, 0), (0, 0)))
        tp = tp.reshape(b, nb + 2, BLOCK, N_KV_HEADS, HEAD_DIM)
        return jnp.concatenate([tp[:, :-2], tp[:, 1:-1], tp[:, 2:]], axis=2)

    kb, vb = band(k), band(v)
    scores = jnp.einsum('bnqhgd,bnkhd->bnhgqk', qb, kb,
                        preferred_element_type=jnp.float32) * (HEAD_DIM ** -0.5)
    qi = jnp.arange(BLOCK)[:, None]
    kj = jnp.arange(3 * BLOCK)[None, :]
    k_pos = (jnp.arange(nb)[:, None, None] - 1) * BLOCK + kj
    mask = (jnp.abs(kj - BLOCK - qi) <= WINDOW) & (k_pos >= 0) & (k_pos < s)
    scores = jnp.where(mask[None, :, None, None], scores, NEG_INF)
    sink_col = jnp.broadcast_to(
        sink.astype(jnp.float32).reshape(N_KV_HEADS, grp)[None, None, :, :, None, None],
        scores.shape[:-1] + (1,))
    probs = jax.nn.softmax(jnp.concatenate([scores, sink_col], axis=-1), axis=-1)[..., :-1]
    out = jnp.einsum('bnhgqk,bnkhd->bnqhgd', probs.astype(v.dtype), vb)
    return out.reshape(b, s, ATTN_WIDTH)


def centred_short_conv(u, w):
    up = jnp.pad(u, ((0, 0), (1, 1), (0, 0)))
    return up[:, :-2] * w[0] + up[:, 1:-1] * w[1] + up[:, 2:] * w[2]


def hybrid_mixer(h, w_in, sink, conv_w, g_attn, g_conv, w_out, cos, sin):
    b, s, _ = h.shape
    proj = h @ w_in
    q, k, v, gate_b, gate_c, u = jnp.split(proj, SPLITS, axis=-1)
    q = partial_rope(q.reshape(b, s, N_HEADS, HEAD_DIM), cos, sin)
    k = partial_rope(k.reshape(b, s, N_KV_HEADS, HEAD_DIM), cos, sin)
    v = v.reshape(b, s, N_KV_HEADS, HEAD_DIM)
    y_attn = windowed_gqa_with_sink(q, k, v, sink)
    y_conv = gate_b * centred_short_conv(gate_c * u, conv_w)
    y = jnp.concatenate([rmsnorm(y_attn, g_attn), rmsnorm(y_conv, g_conv)], axis=-1)
    return y @ w_out


def hierarchical_moe(h, w_rg, b_rg, w_re, b_re, w_gate, w_up, w_down):
    b, s, d = h.shape
    t = b * s
    xf = h.reshape(t, d)
    group_logits = (xf @ w_rg).astype(jnp.float32) + b_rg.astype(jnp.float32)
    group_prob = jax.nn.softmax(group_logits, axis=-1)
    grp = jnp.argmax(group_logits, axis=-1)
    p_grp = jnp.take_along_axis(group_prob, grp[:, None], axis=-1)
    expert_logits = ((xf @ w_re).astype(jnp.float32) + b_re.astype(jnp.float32)
                     ).reshape(t, N_EXPERT_GROUPS, EXPERTS_PER_GROUP)
    in_group = jnp.take_along_axis(expert_logits, grp[:, None, None], axis=1)[:, 0]
    top_logit, top_local = lax.top_k(in_group, TOP_K)
    gates = p_grp * jax.nn.softmax(top_logit, axis=-1)
    experts = grp[:, None] * EXPERTS_PER_GROUP + top_local

    a = t * TOP_K
    flat_e = experts.reshape(a)
    flat_tok = jnp.arange(a, dtype=jnp.int32) // TOP_K
    flat_g = gates.reshape(a)
    order = jnp.argsort(flat_e)
    e_sorted, tok_sorted, g_sorted = flat_e[order], flat_tok[order], flat_g[order]
    counts = jnp.bincount(flat_e, length=N_EXPERTS)
    starts = jnp.cumsum(counts) - counts
    padded = (counts + MOE_BLOCK - 1) // MOE_BLOCK * MOE_BLOCK
    padded_end = jnp.cumsum(padded)
    dest = (padded_end - padded)[e_sorted] + jnp.arange(a) - starts[e_sorted]
    n_blocks = -(-a // MOE_BLOCK) + N_EXPERTS
    cap = n_blocks * MOE_BLOCK
    slot_tok = jnp.zeros((cap,), jnp.int32).at[dest].set(tok_sorted)
    block_expert = jnp.minimum(
        jnp.searchsorted(padded_end, jnp.arange(n_blocks) * MOE_BLOCK, side='right'),
        N_EXPERTS - 1)
    x_slots = xf[slot_tok].reshape(n_blocks, MOE_BLOCK, d)

    def expert_block(args):
        xb, e = args
        return (jax.nn.silu(xb @ w_gate[e]) * (xb @ w_up[e])) @ w_down[e]

    y_slots = lax.map(expert_block, (x_slots, block_expert)).reshape(cap, d)
    y_assign = y_slots[dest] * g_sorted[:, None].astype(h.dtype)
    y = jax.ops.segment_sum(y_assign, tok_sorted, num_segments=t)
    return y.reshape(b, s, d)


def setup_inputs(seed: int = 0) -> dict:
    key = jax.random.key(seed)
    ks = jax.random.split(key, 17)
    f32 = jnp.float32
    nrm = lambda k, shape, scale: jax.random.normal(k, shape, f32) * scale
    res_scale = (2.0 * DEPTH) ** -0.5
    return {
        "x": nrm(ks[0], (BATCH, SEQ, D_MODEL), 1.0),
        "norm_mix": 1.0 + nrm(ks[1], (DEPTH, D_MODEL), 0.02),
        "w_in": nrm(ks[2], (DEPTH, D_MODEL, IN_WIDTH), D_MODEL ** -0.5),
        "attn_sink": nrm(ks[3], (DEPTH, N_HEADS), 0.5),
        "conv_w": nrm(ks[4], (DEPTH, CONV_K, CONV_WIDTH), CONV_K ** -0.5),
        "norm_attn_out": 1.0 + nrm(ks[5], (DEPTH, ATTN_WIDTH), 0.02),
        "norm_conv_out": 1.0 + nrm(ks[6], (DEPTH, CONV_WIDTH), 0.02),
        "w_out": nrm(ks[7], (DEPTH, MIX_WIDTH, D_MODEL), MIX_WIDTH ** -0.5 * res_scale),
        "norm_ffn": 1.0 + nrm(ks[8], (DEPTH, D_MODEL), 0.02),
        "w_router_group": nrm(ks[9], (DEPTH, D_MODEL, N_EXPERT_GROUPS), D_MODEL ** -0.5),
        "b_router_group": nrm(ks[10], (DEPTH, N_EXPERT_GROUPS), 0.01),
        "w_router_expert": nrm(ks[11], (DEPTH, D_MODEL, N_EXPERTS), D_MODEL ** -0.5),
        "b_router_expert": nrm(ks[12], (DEPTH, N_EXPERTS), 0.01),
        "w_expert_gate": nrm(ks[13], (DEPTH, N_EXPERTS, D_MODEL, D_EXPERT), D_MODEL ** -0.5),
        "w_expert_up": nrm(ks[14], (DEPTH, N_EXPERTS, D_MODEL, D_EXPERT), D_MODEL ** -0.5),
        "w_expert_down": nrm(ks[15], (DEPTH, N_EXPERTS, D_EXPERT, D_MODEL), D_EXPERT ** -0.5 * res_scale),
        "norm_final": 1.0 + nrm(ks[16], (D_MODEL,), 0.02),
    }


def reference(x, norm_mix, w_in, attn_sink, conv_w, norm_attn_out, norm_conv_out, w_out,
              norm_ffn, w_router_group, b_router_group, w_router_expert, b_router_expert,
              w_expert_gate, w_expert_up, w_expert_down, norm_final):
    cos, sin = rope_tables(x.shape[1])
    for l in range(DEPTH):
        x = x + hybrid_mixer(rmsnorm(x, norm_mix[l]), w_in[l], attn_sink[l], conv_w[l],
                             norm_attn_out[l], norm_conv_out[l], w_out[l], cos, sin)
        x = x + hierarchical_moe(rmsnorm(x, norm_ffn[l]), w_router_group[l], b_router_group[l],
                                 w_router_expert[l], b_router_expert[l], w_expert_gate[l],
                                 w_expert_up[l], w_expert_down[l])
    return rmsnorm(x, norm_final)
```

```python
import functools

import jax
import jax.numpy as jnp
from jax import lax
from jax.experimental import pallas as pl
from jax.experimental.pallas import tpu as pltpu

F32 = jnp.float32
BF16 = jnp.bfloat16
U32 = jnp.uint32
I32 = jnp.int32

N_HEADS = 8
N_KV_HEADS = 2
HEAD_DIM = 64
ATTN_WIDTH = N_HEADS * HEAD_DIM
KV_WIDTH = N_KV_HEADS * HEAD_DIM
WINDOW = 128
QBLOCK = 128
ROPE_DIM = HEAD_DIM // 4
ROPE_THETA = 500000.0
CONV_WIDTH = 512
N_GROUPS = 4
PER_GROUP = 8
N_EXPERTS = N_GROUPS * PER_GROUP
D_EXPERT = 512
EPS = 1e-6
NEG_INF = -1e30

LANES = 128
SUBLANES = 8

FRONT_ROWS = 512
MIX_ROWS = 512
SORT_ROWS = 256
CHUNK = SUBLANES
BLOCK_ROWS = 256
BLOCK_CHUNKS = BLOCK_ROWS // CHUNK
SORT_CAP = ((2 * SORT_ROWS + N_EXPERTS * (CHUNK - 1)) // LANES + 1) * LANES
TILE_CHUNKS = SORT_CAP // CHUNK
PACK_W = 512
XS_W = PACK_W + LANES

VMEM_LIMIT = 56 * 1024 * 1024


def _rms(x, g):
    return x * lax.rsqrt(jnp.mean(x * x, axis=-1, keepdims=True) + EPS) * g


def _front_kernel(x_ref, g_ref, w_ref, cos_ref, sp_ref, sm_ref,
                  q_ref, k_ref, v_ref, b_ref, cu_ref):
    h = _rms(x_ref[...], g_ref[...]).astype(BF16)
    cos = cos_ref[...]
    sp = sp_ref[...]
    sm = sm_ref[...]

    def proj(lo, hi):
        return jnp.dot(h, w_ref[:, lo:hi], preferred_element_type=F32)

    def rope(t):
        half = ROPE_DIM // 2
        return (t * cos + pltpu.roll(t, half, axis=1) * sp
                + pltpu.roll(t, LANES - half, axis=1) * sm)

    for c in range(ATTN_WIDTH // LANES):
        q = proj(c * LANES, (c + 1) * LANES)
        q_ref[:, c * LANES:(c + 1) * LANES] = (rope(q) * (HEAD_DIM ** -0.5)).astype(BF16)
    o = ATTN_WIDTH
    k_ref[...] = rope(proj(o, o + KV_WIDTH)).astype(BF16)
    o += KV_WIDTH
    v_ref[...] = proj(o, o + KV_WIDTH).astype(BF16)
    o += KV_WIDTH
    b_ref[...] = proj(o, o + CONV_WIDTH).astype(BF16)
    o += CONV_WIDTH
    gc = proj(o, o + CONV_WIDTH)
    o += CONV_WIDTH
    u = proj(o, o + CONV_WIDTH)
    cu_ref[...] = (gc * u).astype(BF16)


def _front(x2, g, w_in_b, cos_t, sp_t, sm_t, seq):
    t, d = x2.shape
    tm = FRONT_ROWS
    nseq = seq // tm
    row = lambda i: (i, 0)
    pos = lambda i: (i % nseq, 0)
    fixed = lambda i: (0, 0)
    return pl.pallas_call(
        _front_kernel,
        grid=(t // tm,),
        in_specs=[
            pl.BlockSpec((tm, d), row),
            pl.BlockSpec((1, d), fixed),
            pl.BlockSpec(w_in_b.shape, fixed),
            pl.BlockSpec((tm, LANES), pos),
            pl.BlockSpec((tm, LANES), pos),
            pl.BlockSpec((tm, LANES), pos),
        ],
        out_specs=[
            pl.BlockSpec((tm, ATTN_WIDTH), row),
            pl.BlockSpec((tm, KV_WIDTH), row),
            pl.BlockSpec((tm, KV_WIDTH), row),
            pl.BlockSpec((tm, CONV_WIDTH), row),
            pl.BlockSpec((tm, CONV_WIDTH), row),
        ],
        out_shape=[
            jax.ShapeDtypeStruct((t, ATTN_WIDTH), BF16),
            jax.ShapeDtypeStruct((t, KV_WIDTH), BF16),
            jax.ShapeDtypeStruct((t, KV_WIDTH), BF16),
            jax.ShapeDtypeStruct((t, CONV_WIDTH), BF16),
            jax.ShapeDtypeStruct((t, CONV_WIDTH), BF16),
        ],
        compiler_params=pltpu.CompilerParams(
            dimension_semantics=("parallel",), vmem_limit_bytes=VMEM_LIMIT),
        name="front",
    )(x2, g, w_in_b, cos_t, sp_t, sm_t)


def _mix_kernel(sink_ref, x_ref, q_ref, kp_ref, kc_ref, kn_ref, vp_ref, vc_ref, vn_ref,
                b_ref, cu_ref, cup_ref, cun_ref, cw_ref, ga_ref, gc_ref, wo_ref,
                o_ref, ya_ref, *, seq):
    tq = x_ref.shape[0]
    i = pl.program_id(1)
    nseq = pl.num_programs(1)
    grp = N_HEADS // N_KV_HEADS
    rows = grp * QBLOCK
    band = 3 * QBLOCK

    kcat = jnp.concatenate([kp_ref[...], kc_ref[...], kn_ref[...]], axis=0)
    vcat = jnp.concatenate([vp_ref[...], vc_ref[...], vn_ref[...]], axis=0)
    kd = [jnp.concatenate([kcat[:, h * HEAD_DIM:(h + 1) * HEAD_DIM]] * 2, axis=1)
          for h in range(N_KV_HEADS)]
    vd = [jnp.concatenate([vcat[:, h * HEAD_DIM:(h + 1) * HEAD_DIM]] * 2, axis=1)
          for h in range(N_KV_HEADS)]

    qi = lax.broadcasted_iota(I32, (rows, band), 0) % QBLOCK
    kj = lax.broadcasted_iota(I32, (rows, band), 1)
    in_band = jnp.abs(kj - QBLOCK - qi) <= WINDOW
    lane = lax.broadcasted_iota(I32, (QBLOCK, LANES), 1)
    low = lane < HEAD_DIM
    hrow = lax.broadcasted_iota(I32, (rows, 1), 0) // QBLOCK

    for jb in range(tq // QBLOCK):
        r0 = jb * QBLOCK
        kpos = kj + (i * tq + r0 - QBLOCK)
        mask = in_band & (kpos >= 0) & (kpos < seq)
        qb = q_ref[r0:r0 + QBLOCK, :]
        for hk in range(N_KV_HEADS):
            parts = []
            for cc in range(2):
                c = 2 * hk + cc
                q2 = qb[:, c * LANES:(c + 1) * LANES]
                parts.append(jnp.where(low, q2, jnp.zeros_like(q2)))
                parts.append(jnp.where(low, jnp.zeros_like(q2), q2))
            qs = jnp.concatenate(parts, axis=0)
            sink = jnp.zeros((rows, 1), F32)
            for g in range(grp):
                sink = jnp.where(hrow == g, sink_ref[hk * grp + g], sink)
            s = lax.dot_general(qs, kd[hk][r0:r0 + band], (((1,), (1,)), ((), ())),
                                preferred_element_type=F32)
            s = jnp.where(mask, s, NEG_INF)
            m = jnp.maximum(jnp.max(s, axis=1, keepdims=True), sink)
            p = jnp.exp(s - m)
            den = jnp.sum(p, axis=1, keepdims=True) + jnp.exp(sink - m)
            o = jnp.dot(p.astype(BF16), vd[hk][r0:r0 + band], preferred_element_type=F32)
            o = o * (1.0 / den)
            for cc in range(2):
                c = 2 * hk + cc
                a = o[(2 * cc) * QBLOCK:(2 * cc + 1) * QBLOCK]
                b = o[(2 * cc + 1) * QBLOCK:(2 * cc + 2) * QBLOCK]
                ya_ref[r0:r0 + QBLOCK, c * LANES:(c + 1) * LANES] = jnp.where(low, a, b)

    cu = cu_ref[...].astype(F32)
    halo = cup_ref.shape[0]
    prev_row = jnp.where(i > 0, cup_ref[halo - 1:halo, :].astype(F32), 0.0)
    next_row = jnp.where(i < nseq - 1, cun_ref[0:1, :].astype(F32), 0.0)
    ridx = lax.broadcasted_iota(I32, cu.shape, 0)
    up = jnp.where(ridx == 0, prev_row, pltpu.roll(cu, 1, axis=0))
    dn = jnp.where(ridx == tq - 1, next_row, pltpu.roll(cu, tq - 1, axis=0))
    conv = up * cw_ref[0:1, :] + cu * cw_ref[1:2, :] + dn * cw_ref[2:3, :]
    yc = _rms(b_ref[...].astype(F32) * conv, gc_ref[...]).astype(BF16)
    ya = _rms(ya_ref[...], ga_ref[...]).astype(BF16)
    o_ref[...] = (x_ref[...]
                  + jnp.dot(ya, wo_ref[:ATTN_WIDTH, :], preferred_element_type=F32)
                  + jnp.dot(yc, wo_ref[ATTN_WIDTH:, :], preferred_element_type=F32))


def _mix(x2, q, k, v, gb, cu, sink, conv_w, g_attn, g_conv, w_out_b, batch, seq):
    t, d = x2.shape
    tq = MIX_ROWS
    nseq = seq // tq
    kb = tq // QBLOCK
    nkb = seq // QBLOCK
    halo = 16
    hb = tq // halo
    nhb = seq // halo

    row = lambda b, i: (b * nseq + i, 0)
    kprev = lambda b, i: (b * nkb + jnp.maximum(i * kb - 1, 0), 0)
    knext = lambda b, i: (b * nkb + jnp.minimum((i + 1) * kb, nkb - 1), 0)
    cprev = lambda b, i: (b * nhb + jnp.maximum(i * hb - 1, 0), 0)
    cnext = lambda b, i: (b * nhb + jnp.minimum((i + 1) * hb, nhb - 1), 0)
    fixed = lambda b, i: (0, 0)
    return pl.pallas_call(
        functools.partial(_mix_kernel, seq=seq),
        grid=(batch, nseq),
        in_specs=[
            pl.BlockSpec(memory_space=pltpu.SMEM),
            pl.BlockSpec((tq, d), row),
            pl.BlockSpec((tq, ATTN_WIDTH), row),
            pl.BlockSpec((QBLOCK, KV_WIDTH), kprev),
            pl.BlockSpec((tq, KV_WIDTH), row),
            pl.BlockSpec((QBLOCK, KV_WIDTH), knext),
            pl.BlockSpec((QBLOCK, KV_WIDTH), kprev),
            pl.BlockSpec((tq, KV_WIDTH), row),
            pl.BlockSpec((QBLOCK, KV_WIDTH), knext),
            pl.BlockSpec((tq, CONV_WIDTH), row),
            pl.BlockSpec((tq, CONV_WIDTH), row),
            pl.BlockSpec((halo, CONV_WIDTH), cprev),
            pl.BlockSpec((halo, CONV_WIDTH), cnext),
            pl.BlockSpec(conv_w.shape, fixed),
            pl.BlockSpec((1, ATTN_WIDTH), fixed),
            pl.BlockSpec((1, CONV_WIDTH), fixed),
            pl.BlockSpec(w_out_b.shape, fixed),
        ],
        out_specs=pl.BlockSpec((tq, d), row),
        out_shape=jax.ShapeDtypeStruct((t, d), F32),
        scratch_shapes=[pltpu.VMEM((tq, ATTN_WIDTH), F32)],
        compiler_params=pltpu.CompilerParams(
            dimension_semantics=("parallel", "parallel"), vmem_limit_bytes=VMEM_LIMIT),
        name="mix",
    )(sink, x2, q, k, k, k, v, v, v, gb, cu, cu, cu, conv_w, g_attn, g_conv, w_out_b)


def _split3(a):
    a1 = a.astype(BF16)
    r = a - a1.astype(F32)
    a2 = r.astype(BF16)
    a3 = (r - a2.astype(F32)).astype(BF16)
    return a1, a2, a3


def _dot_nt_f32(w, h):
    dn = (((1,), (1,)), ((), ()))
    ws = _split3(w)
    hs = _split3(h)
    acc = None
    for a, b in ((2, 0), (1, 1), (0, 2), (1, 0), (0, 1), (0, 0)):
        t = lax.dot_general(ws[a], hs[b], dn, preferred_element_type=F32)
        acc = t if acc is None else acc + t
    return acc


def _route_kernel(x_ref, g_ref, wg_ref, bg_ref, we_ref, be_ref,
                  xs_ref, s_ref, meta_ref):
    tl = x_ref.shape[0]
    h = _rms(x_ref[...], g_ref[...])
    hb = h.astype(BF16)

    gl = _dot_nt_f32(wg_ref[...], h) + bg_ref[...]
    el = _dot_nt_f32(we_ref[...], h) + be_ref[...]

    r8 = lax.broadcasted_iota(I32, gl.shape, 0)
    glm = jnp.where(r8 < N_GROUPS, gl, -jnp.inf)
    gmax = jnp.max(glm, axis=0, keepdims=True)
    grp = jnp.min(jnp.where(glm == gmax, r8, SUBLANES), axis=0, keepdims=True)
    p_grp = 1.0 / jnp.sum(jnp.exp(glm - gmax), axis=0, keepdims=True)

    r32 = lax.broadcasted_iota(I32, el.shape, 0)
    e0 = jnp.where((r32 // PER_GROUP) == grp, el, -jnp.inf)
    t1 = jnp.max(e0, axis=0, keepdims=True)
    i1 = jnp.min(jnp.where(e0 == t1, r32, N_EXPERTS), axis=0, keepdims=True)
    a1 = r32 == i1
    e1 = jnp.where(a1, -jnp.inf, e0)
    t2 = jnp.max(e1, axis=0, keepdims=True)
    i2 = jnp.min(jnp.where(e1 == t2, r32, N_EXPERTS), axis=0, keepdims=True)
    a2 = r32 == i2
    ed = jnp.exp(t2 - t1)
    inv = 1.0 / (1.0 + ed)
    g1 = p_grp * inv
    g2 = p_grp * (ed * inv)

    af = (a1 | a2).astype(F32)
    cnt = jnp.sum(af, axis=1, keepdims=True)
    pc = jnp.floor((cnt + (CHUNK - 1)) * (1.0 / CHUNK)) * CHUNK
    er = lax.broadcasted_iota(I32, (N_EXPERTS, LANES), 0)
    ec = lax.broadcasted_iota(I32, (N_EXPERTS, LANES), 1)
    pc_row = jnp.sum(jnp.where(er == ec, pc, 0.0), axis=0, keepdims=True)
    lo_row = jnp.sum(jnp.where(er < ec, pc, 0.0), axis=0, keepdims=True)
    lo_col = jnp.sum(jnp.where(ec < er, pc_row, 0.0), axis=1, keepdims=True)
    total = jnp.sum(pc, axis=0, keepdims=True)

    tr = lax.broadcasted_iota(I32, (tl, tl), 0)
    tc = lax.broadcasted_iota(I32, (tl, tl), 1)
    upper = (tr <= tc).astype(BF16)
    incl = jnp.dot(af.astype(BF16), upper, preferred_element_type=F32)
    slot = lo_col + incl - af
    s1 = jnp.sum(jnp.where(a1, slot, 0.0), axis=0, keepdims=True).astype(I32)
    s2 = jnp.sum(jnp.where(a2, slot, 0.0), axis=0, keepdims=True).astype(I32)

    cap = xs_ref.shape[1]
    srow = lax.broadcasted_iota(I32, (cap, tl), 0)
    p1 = srow == s1
    p2 = srow == s2
    perm = (p1 | p2).astype(BF16)
    half = hb.shape[1] // 2
    lo = jnp.dot(perm, hb[:, :half], preferred_element_type=F32)
    hi = jnp.dot(perm, hb[:, half:], preferred_element_type=F32)
    lo_bits = lax.bitcast_convert_type(lo, U32) >> 16
    hi_bits = lax.bitcast_convert_type(hi, U32) & jnp.uint32(0xFFFF0000)
    xs_ref[0, :, :PACK_W] = hi_bits | lo_bits
    gs = jnp.sum(jnp.where(p1, g1, 0.0) + jnp.where(p2, g2, 0.0), axis=1, keepdims=True)
    xs_ref[0, :, PACK_W:] = lax.bitcast_convert_type(
        jnp.broadcast_to(gs, (cap, LANES)), U32)

    sr = lax.broadcasted_iota(I32, (SUBLANES, tl), 0)
    s_ref[0] = jnp.where(sr == 0, s1, jnp.where(sr == 1, s2, 0))
    mr = lax.broadcasted_iota(I32, (SUBLANES, LANES), 0)
    meta = jnp.where(mr == 0, pc_row, jnp.where(mr == 1, lo_row, jnp.where(mr == 2, total, 0.0)))
    meta_ref[0] = (meta * (1.0 / CHUNK)).astype(I32)


def _route(x2, g, wg_t, bg, we_t, be):
    t, d = x2.shape
    tl = SORT_ROWS
    nt = t // tl
    fixed = lambda i: (0, 0)
    return pl.pallas_call(
        _route_kernel,
        grid=(nt,),
        in_specs=[
            pl.BlockSpec((tl, d), lambda i: (i, 0)),
            pl.BlockSpec((1, d), fixed),
            pl.BlockSpec(wg_t.shape, fixed),
            pl.BlockSpec(bg.shape, fixed),
            pl.BlockSpec(we_t.shape, fixed),
            pl.BlockSpec(be.shape, fixed),
        ],
        out_specs=[
            pl.BlockSpec((1, SORT_CAP, XS_W), lambda i: (i, 0, 0)),
            pl.BlockSpec((1, SUBLANES, tl), lambda i: (i, 0, 0)),
            pl.BlockSpec((1, SUBLANES, LANES), lambda i: (i, 0, 0)),
        ],
        out_shape=[
            jax.ShapeDtypeStruct((nt, SORT_CAP, XS_W), U32),
            jax.ShapeDtypeStruct((nt, SUBLANES, tl), I32),
            jax.ShapeDtypeStruct((nt, SUBLANES, LANES), I32),
        ],
        compiler_params=pltpu.CompilerParams(
            dimension_semantics=("parallel",), vmem_limit_bytes=VMEM_LIMIT),
        name="route",
    )(x2, g, wg_t, bg, we_t, be)


def _n_blocks(t):
    nt = t // SORT_ROWS
    max_chunks = (2 * t + (CHUNK - 1) * N_EXPERTS * nt) // CHUNK + (BLOCK_CHUNKS - 1) * N_EXPERTS
    return -(-max_chunks // BLOCK_CHUNKS)


def _plan(meta, t):
    nt = t // SORT_ROWS
    nb = _n_blocks(t)
    nch = meta[:, 0, :N_EXPERTS]
    lo8 = meta[:, 1, :N_EXPERTS]
    tc = jnp.sum(nch, axis=0)
    ptc = (tc + BLOCK_CHUNKS - 1) // BLOCK_CHUNKS * BLOCK_CHUNKS
    eend = jnp.cumsum(ptc)
    n_used = (eend[-1] // BLOCK_CHUNKS).astype(I32)
    src0 = (jnp.arange(nt, dtype=I32)[:, None] * TILE_CHUNKS + lo8).T
    pad_src = jnp.full((N_EXPERTS, 1), TILE_CHUNKS - 1, I32)
    run_len = jnp.concatenate([nch.T, (ptc - tc)[:, None]], axis=1).reshape(-1)
    run_src = jnp.concatenate([src0, pad_src], axis=1).reshape(-1)
    run_pad = jnp.concatenate([jnp.zeros((N_EXPERTS, nt), bool),
                               jnp.ones((N_EXPERTS, 1), bool)], axis=1).reshape(-1)
    run_end = jnp.cumsum(run_len)
    run_start = run_end - run_len
    kk = jnp.arange(nb * BLOCK_CHUNKS, dtype=I32)
    rid = jnp.minimum(jnp.searchsorted(run_end, kk, side="right"), run_len.shape[0] - 1)
    pad = run_pad[rid] | (kk >= eend[-1])
    csrc = jnp.where(pad, TILE_CHUNKS - 1, run_src[rid] + kk - run_start[rid]).astype(I32)
    cdst = jnp.where(pad, nt * TILE_CHUNKS + kk, csrc).astype(I32)
    bb = jnp.minimum(jnp.arange(nb, dtype=I32), n_used - 1)
    blk_e = jnp.minimum(jnp.searchsorted(eend // BLOCK_CHUNKS, bb, side="right"),
                        N_EXPERTS - 1).astype(I32)
    return blk_e, n_used.reshape(1), csrc, cdst


def _expert_kernel(be_ref, nu_ref, csrc_ref, cdst_ref,
                   xs_hbm, wg_ref, wu_ref, wd_ref, ys_hbm,
                   xbuf, ybuf, wgb, wub, wdb, sem_in, sem_out):
    b = pl.program_id(0)
    n_used = nu_ref[0]
    slot = b % 2

    def gather(blk, sl):
        return [pltpu.make_async_copy(xs_hbm.at[csrc_ref[blk * BLOCK_CHUNKS + j]],
                                      xbuf.at[sl, j], sem_in.at[sl])
                for j in range(BLOCK_CHUNKS)]

    def scatter(blk, sl):
        return [pltpu.make_async_copy(ybuf.at[sl, j],
                                      ys_hbm.at[cdst_ref[blk * BLOCK_CHUNKS + j]], sem_out.at[sl])
                for j in range(BLOCK_CHUNKS)]

    @pl.when(b == 0)
    def _():
        for c in gather(0, 0):
            c.start()

    @pl.when(b + 1 < n_used)
    def _():
        for c in gather(b + 1, 1 - slot):
            c.start()

    new_expert = jnp.logical_or(b == 0, be_ref[b] != be_ref[jnp.maximum(b - 1, 0)])

    @pl.when(jnp.logical_and(b < n_used, new_expert))
    def _():
        wgb[...] = wg_ref[0].astype(BF16)
        wub[...] = wu_ref[0].astype(BF16)
        wdb[...] = wd_ref[0].astype(BF16)

    @pl.when(b < n_used)
    def _():
        for c in gather(b, slot):
            c.wait()
        w = xbuf[slot].reshape(BLOCK_ROWS, XS_W)
        xw = w[:, :PACK_W]
        lo = lax.bitcast_convert_type(xw << 16, F32).astype(BF16)
        hi = lax.bitcast_convert_type(xw & jnp.uint32(0xFFFF0000), F32).astype(BF16)
        gate = lax.bitcast_convert_type(w[:, PACK_W:], F32)[:, 0:1]
        half = PACK_W
        hg = (jnp.dot(lo, wgb[:half, :], preferred_element_type=F32)
              + jnp.dot(hi, wgb[half:, :], preferred_element_type=F32))
        hu = (jnp.dot(lo, wub[:half, :], preferred_element_type=F32)
              + jnp.dot(hi, wub[half:, :], preferred_element_type=F32))
        act = (hg * (1.0 / (1.0 + jnp.exp(-hg))) * hu).astype(BF16)
        y = jnp.dot(act, wdb[...], preferred_element_type=F32) * gate
        yb = y.astype(BF16).astype(F32)
        lo_bits = lax.bitcast_convert_type(yb[:, :half], U32) >> 16
        hi_bits = lax.bitcast_convert_type(yb[:, half:], U32) & jnp.uint32(0xFFFF0000)
        ybuf[slot] = (hi_bits | lo_bits).reshape(BLOCK_CHUNKS, CHUNK, PACK_W)
        for c in scatter(b, slot):
            c.start()

    @pl.when(jnp.logical_and(b >= 1, b < n_used))
    def _():
        for c in scatter(b - 1, 1 - slot):
            c.wait()

    @pl.when(b == n_used - 1)
    def _():
        for c in scatter(b, slot):
            c.wait()


def _experts(xs, blk_e, n_used, csrc, cdst, w_gate, w_up, w_down, t):
    nt = t // SORT_ROWS
    nb = _n_blocks(t)
    d = w_gate.shape[1]
    xs3 = xs.reshape(nt * TILE_CHUNKS, CHUNK, XS_W)
    n_out = nt * TILE_CHUNKS + nb * BLOCK_CHUNKS
    wmap = lambda b, be, nu, cs, cd: (be[b], 0, 0)
    grid_spec = pltpu.PrefetchScalarGridSpec(
        num_scalar_prefetch=4,
        grid=(nb,),
        in_specs=[
            pl.BlockSpec(memory_space=pl.ANY),
            pl.BlockSpec((1, d, D_EXPERT), wmap),
            pl.BlockSpec((1, d, D_EXPERT), wmap),
            pl.BlockSpec((1, D_EXPERT, d), wmap),
        ],
        out_specs=pl.BlockSpec(memory_space=pl.ANY),
        scratch_shapes=[
            pltpu.VMEM((2, BLOCK_CHUNKS, CHUNK, XS_W), U32),
            pltpu.VMEM((2, BLOCK_CHUNKS, CHUNK, PACK_W), U32),
            pltpu.VMEM((d, D_EXPERT), BF16),
            pltpu.VMEM((d, D_EXPERT), BF16),
            pltpu.VMEM((D_EXPERT, d), BF16),
            pltpu.SemaphoreType.DMA((2,)),
            pltpu.SemaphoreType.DMA((2,)),
        ],
    )
    return pl.pallas_call(
        _expert_kernel,
        grid_spec=grid_spec,
        out_shape=jax.ShapeDtypeStruct((n_out, CHUNK, PACK_W), U32),
        compiler_params=pltpu.CompilerParams(
            dimension_semantics=("arbitrary",), vmem_limit_bytes=VMEM_LIMIT,
            has_side_effects=True),
        name="experts",
    )(blk_e, n_used, csrc, cdst, xs3, w_gate, w_up, w_down)


def _combine_kernel(x_ref, ys_ref, s_ref, meta_ref, g_ref, o_ref, *, final_norm):
    tl = x_ref.shape[0]
    cap = ys_ref.shape[0]
    used = meta_ref[0, 2:3, 0:1] * CHUNK
    yrow = lax.broadcasted_iota(I32, ys_ref.shape, 0)
    ys = jnp.where(yrow < used, ys_ref[...], jnp.uint32(0))
    ylo = lax.bitcast_convert_type(ys << 16, F32).astype(BF16)
    yhi = lax.bitcast_convert_type(ys & jnp.uint32(0xFFFF0000), F32).astype(BF16)

    tr = lax.broadcasted_iota(I32, (tl, tl), 0)
    tc = lax.broadcasted_iota(I32, (tl, tl), 1)
    eye = tr == tc
    s = s_ref[0]
    s1 = jnp.sum(jnp.where(eye, s[0:1, :], 0), axis=1, keepdims=True)
    s2 = jnp.sum(jnp.where(eye, s[1:2, :], 0), axis=1, keepdims=True)
    col = lax.broadcasted_iota(I32, (tl, cap), 1)
    perm_t = ((col == s1) | (col == s2)).astype(BF16)
    half = ylo.shape[1]
    out = jnp.concatenate(
        [x_ref[:, :half] + jnp.dot(perm_t, ylo, preferred_element_type=F32),
         x_ref[:, half:] + jnp.dot(perm_t, yhi, preferred_element_type=F32)], axis=1)
    if final_norm:
        out = _rms(out, g_ref[...])
    o_ref[...] = out


def _combine(x2, ys, s12, meta, g_final, final_norm):
    t, d = x2.shape
    tl = SORT_ROWS
    nt = t // tl
    ys2 = ys.reshape(ys.shape[0] * CHUNK, PACK_W)
    return pl.pallas_call(
        functools.partial(_combine_kernel, final_norm=final_norm),
        grid=(nt,),
        in_specs=[
            pl.BlockSpec((tl, d), lambda i: (i, 0)),
            pl.BlockSpec((SORT_CAP, PACK_W), lambda i: (i, 0)),
            pl.BlockSpec((1, SUBLANES, tl), lambda i: (i, 0, 0)),
            pl.BlockSpec((1, SUBLANES, LANES), lambda i: (i, 0, 0)),
            pl.BlockSpec((1, d), lambda i: (0, 0)),
        ],
        out_specs=pl.BlockSpec((tl, d), lambda i: (i, 0)),
        out_shape=jax.ShapeDtypeStruct((t, d), F32),
        compiler_params=pltpu.CompilerParams(
            dimension_semantics=("parallel",), vmem_limit_bytes=VMEM_LIMIT),
        name="combine",
    )(x2, ys2, s12, meta, g_final)


def _rope_tables(seq):
    pos = jnp.arange(seq, dtype=F32)
    inv_freq = ROPE_THETA ** (-jnp.arange(0, ROPE_DIM, 2, dtype=F32) / ROPE_DIM)
    ang = pos[:, None] * inv_freq[None, :]
    cos, sin = jnp.cos(ang), jnp.sin(ang)
    half = ROPE_DIM // 2
    ones = jnp.ones((seq, HEAD_DIM - ROPE_DIM), F32)
    zeros_h = jnp.zeros((seq, half), F32)
    zeros_r = jnp.zeros((seq, HEAD_DIM - ROPE_DIM), F32)
    cos_h = jnp.concatenate([cos, cos, ones], axis=1)
    sp_h = jnp.concatenate([zeros_h, sin, zeros_r], axis=1)
    sm_h = jnp.concatenate([-sin, zeros_h, zeros_r], axis=1)
    rep = LANES // HEAD_DIM
    return (jnp.tile(cos_h, (1, rep)), jnp.tile(sp_h, (1, rep)), jnp.tile(sm_h, (1, rep)))


def kernel(x, norm_mix, w_in, attn_sink, conv_w, norm_attn_out, norm_conv_out, w_out, norm_ffn,
           w_router_group, b_router_group, w_router_expert, b_router_expert,
           w_expert_gate, w_expert_up, w_expert_down, norm_final):
    batch, seq, d = x.shape
    depth = w_in.shape[0]
    t = batch * seq
    assert seq % MIX_ROWS == 0 and seq % FRONT_ROWS == 0 and t % SORT_ROWS == 0
    cos_t, sp_t, sm_t = _rope_tables(seq)
    x2 = x.reshape(t, d)
    w_in_b = w_in.astype(BF16)
    w_out_b = w_out.astype(BF16)
    wg_t = jnp.pad(jnp.swapaxes(w_router_group, 1, 2), ((0, 0), (0, SUBLANES - N_GROUPS), (0, 0)))
    bg = jnp.pad(b_router_group, ((0, 0), (0, SUBLANES - N_GROUPS)))[:, :, None]
    we_t = jnp.swapaxes(w_router_expert, 1, 2)
    be = b_router_expert[:, :, None]
    g_final = norm_final.reshape(1, d)

    for l in range(depth):
        q, k, v, gb, cu = _front(x2, norm_mix[l].reshape(1, d), w_in_b[l], cos_t, sp_t, sm_t, seq)
        x2 = _mix(x2, q, k, v, gb, cu, attn_sink[l], conv_w[l],
                  norm_attn_out[l].reshape(1, -1), norm_conv_out[l].reshape(1, -1),
                  w_out_b[l], batch, seq)
        xs, s12, meta = _route(x2, norm_ffn[l].reshape(1, d), wg_t[l], bg[l], we_t[l], be[l])
        blk_e, n_used, csrc, cdst = _plan(meta, t)
        ys = _experts(xs, blk_e, n_used, csrc, cdst,
                      w_expert_gate[l], w_expert_up[l], w_expert_down[l], t)
        x2 = _combine(x2, ys, s12, meta, g_final, final_norm=(l == depth - 1))
    return x2.reshape(batch, seq, d)
```

```python
import functools

import jax
import jax.numpy as jnp
from jax import lax
from jax.experimental import pallas as pl
from jax.experimental.pallas import tpu as pltpu

F32 = jnp.float32
BF16 = jnp.bfloat16
U32 = jnp.uint32
I32 = jnp.int32

N_HEADS = 8
N_KV_HEADS = 2
HEAD_DIM = 64
ATTN_WIDTH = N_HEADS * HEAD_DIM
KV_WIDTH = N_KV_HEADS * HEAD_DIM
WINDOW = 128
QBLOCK = 128
ROPE_DIM = HEAD_DIM // 4
ROPE_THETA = 500000.0
CONV_WIDTH = 512
N_GROUPS = 4
PER_GROUP = 8
N_EXPERTS = N_GROUPS * PER_GROUP
D_EXPERT = 512
EPS = 1e-6
NEG_INF = -1e30

LANES = 128
SUBLANES = 8

FRONT_ROWS = 512
MIX_ROWS = 512
SORT_ROWS = 256
CHUNK = SUBLANES
BLOCK_ROWS = 256
BLOCK_CHUNKS = BLOCK_ROWS // CHUNK
SORT_CAP = ((2 * SORT_ROWS + N_EXPERTS * (CHUNK - 1)) // LANES + 1) * LANES
TILE_CHUNKS = SORT_CAP // CHUNK
PACK_W = 512
XS_W = PACK_W + LANES

VMEM_LIMIT = 56 * 1024 * 1024


def _rms(x, g):
    return x * lax.rsqrt(jnp.mean(x * x, axis=-1, keepdims=True) + EPS) * g


def _pack2(lo, hi):
    return pltpu.pack_elementwise([lo, hi], packed_dtype=BF16)


def _unpack2(w, index):
    return pltpu.unpack_elementwise(w, index=index, packed_dtype=BF16, unpacked_dtype=F32)


def _front_kernel(x_ref, g_ref, w_ref, cos_ref, sp_ref, sm_ref,
                  q_ref, k_ref, v_ref, b_ref, cu_ref):
    h = _rms(x_ref[...], g_ref[...]).astype(BF16)
    cos = cos_ref[...]
    sp = sp_ref[...]
    sm = sm_ref[...]

    def proj(lo, hi):
        return jnp.dot(h, w_ref[:, lo:hi], preferred_element_type=F32)

    def rope(t):
        half = ROPE_DIM // 2
        return (t * cos + pltpu.roll(t, half, axis=1) * sp
                + pltpu.roll(t, LANES - half, axis=1) * sm)

    for c in range(ATTN_WIDTH // LANES):
        q = proj(c * LANES, (c + 1) * LANES)
        q_ref[:, c * LANES:(c + 1) * LANES] = (rope(q) * (HEAD_DIM ** -0.5)).astype(BF16)
    o = ATTN_WIDTH
    k_ref[...] = rope(proj(o, o + KV_WIDTH)).astype(BF16)
    o += KV_WIDTH
    v_ref[...] = proj(o, o + KV_WIDTH).astype(BF16)
    o += KV_WIDTH
    b_ref[...] = proj(o, o + CONV_WIDTH).astype(BF16)
    o += CONV_WIDTH
    gc = proj(o, o + CONV_WIDTH)
    o += CONV_WIDTH
    u = proj(o, o + CONV_WIDTH)
    cu_ref[...] = (gc * u).astype(BF16)


def _front(x2, g, w_in_b, cos_t, sp_t, sm_t, seq):
    t, d = x2.shape
    tm = FRONT_ROWS
    nseq = seq // tm
    row = lambda i: (i, 0)
    pos = lambda i: (i % nseq, 0)
    fixed = lambda i: (0, 0)
    return pl.pallas_call(
        _front_kernel,
        grid=(t // tm,),
        in_specs=[
            pl.BlockSpec((tm, d), row),
            pl.BlockSpec((1, d), fixed),
            pl.BlockSpec(w_in_b.shape, fixed),
            pl.BlockSpec((tm, LANES), pos),
            pl.BlockSpec((tm, LANES), pos),
            pl.BlockSpec((tm, LANES), pos),
        ],
        out_specs=[
            pl.BlockSpec((tm, ATTN_WIDTH), row),
            pl.BlockSpec((tm, KV_WIDTH), row),
            pl.BlockSpec((tm, KV_WIDTH), row),
            pl.BlockSpec((tm, CONV_WIDTH), row),
            pl.BlockSpec((tm, CONV_WIDTH), row),
        ],
        out_shape=[
            jax.ShapeDtypeStruct((t, ATTN_WIDTH), BF16),
            jax.ShapeDtypeStruct((t, KV_WIDTH), BF16),
            jax.ShapeDtypeStruct((t, KV_WIDTH), BF16),
            jax.ShapeDtypeStruct((t, CONV_WIDTH), BF16),
            jax.ShapeDtypeStruct((t, CONV_WIDTH), BF16),
        ],
        compiler_params=pltpu.CompilerParams(
            dimension_semantics=("parallel",), vmem_limit_bytes=VMEM_LIMIT),
        name="front",
    )(x2, g, w_in_b, cos_t, sp_t, sm_t)


def _mix_kernel(sink_ref, x_ref, q_ref, kp_ref, kc_ref, kn_ref, vp_ref, vc_ref, vn_ref,
                b_ref, cu_ref, cup_ref, cun_ref, cw_ref, ga_ref, gc_ref, wo_ref,
                o_ref, ya_ref, *, seq):
    tq = x_ref.shape[0]
    i = pl.program_id(1)
    nseq = pl.num_programs(1)
    grp = N_HEADS // N_KV_HEADS
    rows = grp * QBLOCK
    band = 3 * QBLOCK

    kcat = jnp.concatenate([kp_ref[...], kc_ref[...], kn_ref[...]], axis=0)
    vcat = jnp.concatenate([vp_ref[...], vc_ref[...], vn_ref[...]], axis=0)
    kd = [jnp.concatenate([kcat[:, h * HEAD_DIM:(h + 1) * HEAD_DIM]] * 2, axis=1)
          for h in range(N_KV_HEADS)]
    vd = [jnp.concatenate([vcat[:, h * HEAD_DIM:(h + 1) * HEAD_DIM]] * 2, axis=1)
          for h in range(N_KV_HEADS)]

    qi = lax.broadcasted_iota(I32, (rows, band), 0) % QBLOCK
    kj = lax.broadcasted_iota(I32, (rows, band), 1)
    in_band = jnp.abs(kj - QBLOCK - qi) <= WINDOW
    lane = lax.broadcasted_iota(I32, (QBLOCK, LANES), 1)
    low = lane < HEAD_DIM
    hrow = lax.broadcasted_iota(I32, (rows, 1), 0) // QBLOCK

    for jb in range(tq // QBLOCK):
        r0 = jb * QBLOCK
        kpos = kj + (i * tq + r0 - QBLOCK)
        mask = in_band & (kpos >= 0) & (kpos < seq)
        qb = q_ref[r0:r0 + QBLOCK, :]
        for hk in range(N_KV_HEADS):
            parts = []
            for cc in range(2):
                c = 2 * hk + cc
                q2 = qb[:, c * LANES:(c + 1) * LANES]
                parts.append(jnp.where(low, q2, jnp.zeros_like(q2)))
                parts.append(jnp.where(low, jnp.zeros_like(q2), q2))
            qs = jnp.concatenate(parts, axis=0)
            sink = jnp.zeros((rows, 1), F32)
            for g in range(grp):
                sink = jnp.where(hrow == g, sink_ref[hk * grp + g], sink)
            s = lax.dot_general(qs, kd[hk][r0:r0 + band], (((1,), (1,)), ((), ())),
                                preferred_element_type=F32)
            s = jnp.where(mask, s, NEG_INF)
            m = jnp.maximum(jnp.max(s, axis=1, keepdims=True), sink)
            p = jnp.exp(s - m)
            den = jnp.sum(p, axis=1, keepdims=True) + jnp.exp(sink - m)
            o = jnp.dot(p.astype(BF16), vd[hk][r0:r0 + band], preferred_element_type=F32)
            o = o * (1.0 / den)
            for cc in range(2):
                c = 2 * hk + cc
                a = o[(2 * cc) * QBLOCK:(2 * cc + 1) * QBLOCK]
                b = o[(2 * cc + 1) * QBLOCK:(2 * cc + 2) * QBLOCK]
                ya_ref[r0:r0 + QBLOCK, c * LANES:(c + 1) * LANES] = jnp.where(low, a, b)

    cu = cu_ref[...].astype(F32)
    halo = cup_ref.shape[0]
    prev_row = jnp.where(i > 0, cup_ref[halo - 1:halo, :].astype(F32), 0.0)
    next_row = jnp.where(i < nseq - 1, cun_ref[0:1, :].astype(F32), 0.0)
    ridx = lax.broadcasted_iota(I32, cu.shape, 0)
    up = jnp.where(ridx == 0, prev_row, pltpu.roll(cu, 1, axis=0))
    dn = jnp.where(ridx == tq - 1, next_row, pltpu.roll(cu, tq - 1, axis=0))
    conv = up * cw_ref[0:1, :] + cu * cw_ref[1:2, :] + dn * cw_ref[2:3, :]
    yc = _rms(b_ref[...].astype(F32) * conv, gc_ref[...]).astype(BF16)
    ya = _rms(ya_ref[...], ga_ref[...]).astype(BF16)
    o_ref[...] = (x_ref[...]
                  + jnp.dot(ya, wo_ref[:ATTN_WIDTH, :], preferred_element_type=F32)
                  + jnp.dot(yc, wo_ref[ATTN_WIDTH:, :], preferred_element_type=F32))


def _mix(x2, q, k, v, gb, cu, sink, conv_w, g_attn, g_conv, w_out_b, batch, seq):
    t, d = x2.shape
    tq = MIX_ROWS
    nseq = seq // tq
    kb = tq // QBLOCK
    nkb = seq // QBLOCK
    halo = 16
    hb = tq // halo
    nhb = seq // halo

    row = lambda b, i: (b * nseq + i, 0)
    kprev = lambda b, i: (b * nkb + jnp.maximum(i * kb - 1, 0), 0)
    knext = lambda b, i: (b * nkb + jnp.minimum((i + 1) * kb, nkb - 1), 0)
    cprev = lambda b, i: (b * nhb + jnp.maximum(i * hb - 1, 0), 0)
    cnext = lambda b, i: (b * nhb + jnp.minimum((i + 1) * hb, nhb - 1), 0)
    fixed = lambda b, i: (0, 0)
    return pl.pallas_call(
        functools.partial(_mix_kernel, seq=seq),
        grid=(batch, nseq),
        in_specs=[
            pl.BlockSpec(memory_space=pltpu.SMEM),
            pl.BlockSpec((tq, d), row),
            pl.BlockSpec((tq, ATTN_WIDTH), row),
            pl.BlockSpec((QBLOCK, KV_WIDTH), kprev),
            pl.BlockSpec((tq, KV_WIDTH), row),
            pl.BlockSpec((QBLOCK, KV_WIDTH), knext),
            pl.BlockSpec((QBLOCK, KV_WIDTH), kprev),
            pl.BlockSpec((tq, KV_WIDTH), row),
            pl.BlockSpec((QBLOCK, KV_WIDTH), knext),
            pl.BlockSpec((tq, CONV_WIDTH), row),
            pl.BlockSpec((tq, CONV_WIDTH), row),
            pl.BlockSpec((halo, CONV_WIDTH), cprev),
            pl.BlockSpec((halo, CONV_WIDTH), cnext),
            pl.BlockSpec(conv_w.shape, fixed),
            pl.BlockSpec((1, ATTN_WIDTH), fixed),
            pl.BlockSpec((1, CONV_WIDTH), fixed),
            pl.BlockSpec(w_out_b.shape, fixed),
        ],
        out_specs=pl.BlockSpec((tq, d), row),
        out_shape=jax.ShapeDtypeStruct((t, d), F32),
        scratch_shapes=[pltpu.VMEM((tq, ATTN_WIDTH), F32)],
        compiler_params=pltpu.CompilerParams(
            dimension_semantics=("parallel", "parallel"), vmem_limit_bytes=VMEM_LIMIT),
        name="mix",
    )(sink, x2, q, k, k, k, v, v, v, gb, cu, cu, cu, conv_w, g_attn, g_conv, w_out_b)


ROUTER_ROWS = N_EXPERTS + SUBLANES


def _route_kernel(x_ref, g_ref, w_ref, b_ref, xs_ref, s_ref, meta_ref):
    tl = x_ref.shape[0]
    h = _rms(x_ref[...], g_ref[...])
    hb = h.astype(BF16)

    dn = (((1,), (1,)), ((), ()))
    h2 = (h - hb.astype(F32)).astype(BF16)
    both = lax.dot_general(w_ref[...], hb, dn, preferred_element_type=F32)
    corr = lax.dot_general(w_ref[:ROUTER_ROWS, :], h2, dn, preferred_element_type=F32)
    logits = both[:ROUTER_ROWS] + both[ROUTER_ROWS:] + corr + b_ref[...]
    el = logits[:N_EXPERTS]
    gl = logits[N_EXPERTS:]

    r8 = lax.broadcasted_iota(I32, gl.shape, 0)
    glm = jnp.where(r8 < N_GROUPS, gl, -jnp.inf)
    gmax = jnp.max(glm, axis=0, keepdims=True)
    grp = jnp.min(jnp.where(glm == gmax, r8, SUBLANES), axis=0, keepdims=True)
    p_grp = 1.0 / jnp.sum(jnp.exp(glm - gmax), axis=0, keepdims=True)

    r32 = lax.broadcasted_iota(I32, el.shape, 0)
    e0 = jnp.where((r32 // PER_GROUP) == grp, el, -jnp.inf)
    t1 = jnp.max(e0, axis=0, keepdims=True)
    i1 = jnp.min(jnp.where(e0 == t1, r32, N_EXPERTS), axis=0, keepdims=True)
    a1 = r32 == i1
    e1 = jnp.where(a1, -jnp.inf, e0)
    t2 = jnp.max(e1, axis=0, keepdims=True)
    i2 = jnp.min(jnp.where(e1 == t2, r32, N_EXPERTS), axis=0, keepdims=True)
    a2 = r32 == i2
    ed = jnp.exp(t2 - t1)
    inv = 1.0 / (1.0 + ed)
    g1 = p_grp * inv
    g2 = p_grp * (ed * inv)

    af = (a1 | a2).astype(F32)
    cnt = jnp.sum(af, axis=1, keepdims=True)
    pc = jnp.floor((cnt + (CHUNK - 1)) * (1.0 / CHUNK)) * CHUNK
    er = lax.broadcasted_iota(I32, (N_EXPERTS, LANES), 0)
    ec = lax.broadcasted_iota(I32, (N_EXPERTS, LANES), 1)
    pc_row = jnp.sum(jnp.where(er == ec, pc, 0.0), axis=0, keepdims=True)
    lo_row = jnp.sum(jnp.where(er < ec, pc, 0.0), axis=0, keepdims=True)
    lo_col = jnp.sum(jnp.where(ec < er, pc_row, 0.0), axis=1, keepdims=True)
    total = jnp.sum(pc, axis=0, keepdims=True)

    tr = lax.broadcasted_iota(I32, (tl, tl), 0)
    tc = lax.broadcasted_iota(I32, (tl, tl), 1)
    upper = (tr <= tc).astype(BF16)
    incl = jnp.dot(af.astype(BF16), upper, preferred_element_type=F32)
    slot = lo_col + incl - af
    s1 = jnp.sum(jnp.where(a1, slot, 0.0), axis=0, keepdims=True).astype(I32)
    s2 = jnp.sum(jnp.where(a2, slot, 0.0), axis=0, keepdims=True).astype(I32)

    cap = xs_ref.shape[1]
    srow = lax.broadcasted_iota(I32, (cap, tl), 0)
    p1 = srow == s1
    p2 = srow == s2
    perm = (p1 | p2).astype(BF16)
    half = hb.shape[1] // 2
    lo = jnp.dot(perm, hb[:, :half], preferred_element_type=F32)
    hi = jnp.dot(perm, hb[:, half:], preferred_element_type=F32)
    xs_ref[0, :, :PACK_W] = _pack2(lo, hi)
    gs = jnp.sum(jnp.where(p1, g1, 0.0) + jnp.where(p2, g2, 0.0), axis=1, keepdims=True)
    gs = jnp.broadcast_to(gs, (cap, LANES))
    gs_hi = gs.astype(BF16).astype(F32)
    xs_ref[0, :, PACK_W:] = _pack2(gs_hi, gs - gs_hi)

    sr = lax.broadcasted_iota(I32, (SUBLANES, tl), 0)
    s_ref[0] = jnp.where(sr == 0, s1, jnp.where(sr == 1, s2, 0))
    mr = lax.broadcasted_iota(I32, (SUBLANES, LANES), 0)
    meta = jnp.where(mr == 0, pc_row, jnp.where(mr == 1, lo_row, jnp.where(mr == 2, total, 0.0)))
    meta_ref[0] = (meta * (1.0 / CHUNK)).astype(I32)


def _router_params(w_group, b_group, w_expert, b_expert):
    pad = SUBLANES - N_GROUPS
    w = jnp.concatenate([jnp.swapaxes(w_expert, 1, 2),
                         jnp.pad(jnp.swapaxes(w_group, 1, 2), ((0, 0), (0, pad), (0, 0)))], axis=1)
    w1 = w.astype(BF16)
    w2 = (w - w1.astype(F32)).astype(BF16)
    b = jnp.concatenate([b_expert, jnp.pad(b_group, ((0, 0), (0, pad)))], axis=1)[:, :, None]
    return jnp.concatenate([w1, w2], axis=1), b


def _route(x2, g, w_router, b_router):
    t, d = x2.shape
    tl = SORT_ROWS
    nt = t // tl
    fixed = lambda i: (0, 0)
    return pl.pallas_call(
        _route_kernel,
        grid=(nt,),
        in_specs=[
            pl.BlockSpec((tl, d), lambda i: (i, 0)),
            pl.BlockSpec((1, d), fixed),
            pl.BlockSpec(w_router.shape, fixed),
            pl.BlockSpec(b_router.shape, fixed),
        ],
        out_specs=[
            pl.BlockSpec((1, SORT_CAP, XS_W), lambda i: (i, 0, 0)),
            pl.BlockSpec((1, SUBLANES, tl), lambda i: (i, 0, 0)),
            pl.BlockSpec((1, SUBLANES, LANES), lambda i: (i, 0, 0)),
        ],
        out_shape=[
            jax.ShapeDtypeStruct((nt, SORT_CAP, XS_W), U32),
            jax.ShapeDtypeStruct((nt, SUBLANES, tl), I32),
            jax.ShapeDtypeStruct((nt, SUBLANES, LANES), I32),
        ],
        compiler_params=pltpu.CompilerParams(
            dimension_semantics=("parallel",), vmem_limit_bytes=VMEM_LIMIT),
        name="route",
    )(x2, g, w_router, b_router)


def _n_blocks(t):
    nt = t // SORT_ROWS
    max_chunks = (2 * t + (CHUNK - 1) * N_EXPERTS * nt) // CHUNK + (BLOCK_CHUNKS - 1) * N_EXPERTS
    return -(-max_chunks // BLOCK_CHUNKS)


PLAN_LANES = 1024


def _list_len(t):
    return -(-_n_blocks(t) * BLOCK_CHUNKS // PLAN_LANES) * PLAN_LANES


def _plan_kernel(nch_ref, ncht_ref, lo8_ref, csrc_ref, cdst_ref, blk_ref, nu_ref):
    nt = nch_ref.shape[0]
    ch = csrc_ref.shape[1]
    nch = nch_ref[:, :N_EXPERTS].astype(F32)
    lo8 = lo8_ref[:, :N_EXPERTS].astype(F32)
    ncht = ncht_ref[...].astype(F32)

    tc = jnp.sum(ncht, axis=1, keepdims=True)
    ptc = jnp.floor((tc + (BLOCK_CHUNKS - 1)) * (1.0 / BLOCK_CHUNKS)) * BLOCK_CHUNKS
    er = lax.broadcasted_iota(I32, (N_EXPERTS, LANES), 0)
    ec = lax.broadcasted_iota(I32, (N_EXPERTS, LANES), 1)
    ptc_row = jnp.sum(jnp.where(er == ec, ptc, 0.0), axis=0, keepdims=True)
    eend = jnp.sum(jnp.where(ec <= er, ptc_row, 0.0), axis=1, keepdims=True)
    estart = eend - ptc
    total = jnp.sum(ptc, axis=0, keepdims=True)

    tr = lax.broadcasted_iota(I32, (nt, nt), 0)
    tcol = lax.broadcasted_iota(I32, (nt, nt), 1)
    lower = (tcol <= tr).astype(BF16)
    cum = jnp.dot(lower, nch.astype(BF16), preferred_element_type=F32)
    cum_hi = jnp.floor(cum * (1.0 / LANES))
    cum_lo = cum - cum_hi * LANES

    k = (pl.program_id(0) * ch + lax.broadcasted_iota(I32, (1, ch), 1)).astype(F32)
    r32 = lax.broadcasted_iota(I32, (N_EXPERTS, ch), 0).astype(F32)

    def expert_of(pos):
        return jnp.minimum(jnp.sum((eend <= pos).astype(F32), axis=0, keepdims=True),
                           N_EXPERTS - 1.0)

    e_k = expert_of(k)
    sel_e = r32 == e_k
    oh_e = sel_e.astype(BF16)
    start_k = jnp.sum(jnp.where(sel_e, estart, 0.0), axis=0, keepdims=True)
    tc_k = jnp.sum(jnp.where(sel_e, tc, 0.0), axis=0, keepdims=True)
    q = k - start_k

    def pick(tbl):
        return jnp.dot(tbl.astype(BF16), oh_e, preferred_element_type=F32)

    cum_e = pick(cum_hi) * LANES + pick(cum_lo)
    nch_e = pick(nch)
    lo8_e = pick(lo8)
    rnt = lax.broadcasted_iota(I32, (nt, ch), 0).astype(F32)
    i_k = jnp.minimum(jnp.sum((cum_e <= q).astype(F32), axis=0, keepdims=True), nt - 1.0)
    sel_i = rnt == i_k
    before = jnp.sum(jnp.where(sel_i, cum_e - nch_e, 0.0), axis=0, keepdims=True)
    lo8_k = jnp.sum(jnp.where(sel_i, lo8_e, 0.0), axis=0, keepdims=True)
    src = i_k * TILE_CHUNKS + lo8_k + (q - before)
    pad = (q >= tc_k) | (k >= total)
    csrc_ref[...] = jnp.where(pad, TILE_CHUNKS - 1.0, src).astype(I32)
    cdst_ref[...] = jnp.where(pad, nt * TILE_CHUNKS + k, src).astype(I32)

    n_used = total * (1.0 / BLOCK_CHUNKS)
    bpos = lax.broadcasted_iota(I32, blk_ref.shape, 1).astype(F32)
    blk_ref[...] = expert_of(jnp.minimum(bpos, n_used - 1.0) * BLOCK_CHUNKS).astype(I32)
    nu_ref[...] = jnp.broadcast_to(n_used, nu_ref.shape).astype(I32)


def _plan(meta, t):
    nt = t // SORT_ROWS
    nl = _list_len(t)
    nbp = -(-_n_blocks(t) // LANES) * LANES
    nch = meta[:, 0, :]
    lo8 = meta[:, 1, :]
    ncht = nch[:, :N_EXPERTS].T
    fixed = lambda c: (0, 0)
    csrc, cdst, blk_e, n_used = pl.pallas_call(
        _plan_kernel,
        grid=(nl // PLAN_LANES,),
        in_specs=[
            pl.BlockSpec(nch.shape, fixed),
            pl.BlockSpec(ncht.shape, fixed),
            pl.BlockSpec(lo8.shape, fixed),
        ],
        out_specs=[
            pl.BlockSpec((1, PLAN_LANES), lambda c: (0, c)),
            pl.BlockSpec((1, PLAN_LANES), lambda c: (0, c)),
            pl.BlockSpec((1, nbp), fixed),
            pl.BlockSpec((1, LANES), fixed),
        ],
        out_shape=[
            jax.ShapeDtypeStruct((1, nl), I32),
            jax.ShapeDtypeStruct((1, nl), I32),
            jax.ShapeDtypeStruct((1, nbp), I32),
            jax.ShapeDtypeStruct((1, LANES), I32),
        ],
        compiler_params=pltpu.CompilerParams(dimension_semantics=("arbitrary",)),
        name="plan",
    )(nch, ncht, lo8)
    return blk_e.reshape(-1), n_used.reshape(-1), csrc.reshape(-1), cdst.reshape(-1)


def _expert_kernel(be_ref, nu_ref, csrc_ref, cdst_ref,
                   xs_hbm, wg_ref, wu_ref, wd_ref, ys_hbm,
                   xbuf, ybuf, wgb, wub, wdb, sem_in, sem_out):
    b = pl.program_id(0)
    n_used = nu_ref[0]
    slot = b % 2

    def gather(blk, sl):
        return [pltpu.make_async_copy(xs_hbm.at[csrc_ref[blk * BLOCK_CHUNKS + j]],
                                      xbuf.at[sl, j], sem_in.at[sl])
                for j in range(BLOCK_CHUNKS)]

    def scatter(blk, sl):
        return [pltpu.make_async_copy(ybuf.at[sl, j],
                                      ys_hbm.at[cdst_ref[blk * BLOCK_CHUNKS + j]], sem_out.at[sl])
                for j in range(BLOCK_CHUNKS)]

    @pl.when(b == 0)
    def _():
        for c in gather(0, 0):
            c.start()

    @pl.when(b + 1 < n_used)
    def _():
        for c in gather(b + 1, 1 - slot):
            c.start()

    new_expert = jnp.logical_or(b == 0, be_ref[b] != be_ref[jnp.maximum(b - 1, 0)])

    @pl.when(jnp.logical_and(b < n_used, new_expert))
    def _():
        wgb[...] = wg_ref[0, 0].astype(BF16)
        wub[...] = wu_ref[0, 0].astype(BF16)
        wdb[...] = wd_ref[0, 0].astype(BF16)

    @pl.when(b < n_used)
    def _():
        for c in gather(b, slot):
            c.wait()
        w = xbuf[slot].reshape(BLOCK_ROWS, XS_W)
        xw = w[:, :PACK_W]
        lo = _unpack2(xw, 0).astype(BF16)
        hi = _unpack2(xw, 1).astype(BF16)
        gw = w[:, PACK_W:]
        gate = (_unpack2(gw, 0) + _unpack2(gw, 1))[:, 0:1]
        half = PACK_W
        hg = (jnp.dot(lo, wgb[:half, :], preferred_element_type=F32)
              + jnp.dot(hi, wgb[half:, :], preferred_element_type=F32))
        hu = (jnp.dot(lo, wub[:half, :], preferred_element_type=F32)
              + jnp.dot(hi, wub[half:, :], preferred_element_type=F32))
        act = (hg * (1.0 / (1.0 + jnp.exp(-hg))) * hu).astype(BF16)
        y = jnp.dot(act, wdb[...], preferred_element_type=F32) * gate
        ybuf[slot] = _pack2(y[:, :half], y[:, half:]).reshape(BLOCK_CHUNKS, CHUNK, PACK_W)
        for c in scatter(b, slot):
            c.start()

    @pl.when(jnp.logical_and(b >= 1, b < n_used))
    def _():
        for c in scatter(b - 1, 1 - slot):
            c.wait()

    @pl.when(b == n_used - 1)
    def _():
        for c in scatter(b, slot):
            c.wait()


def _experts(xs, blk_e, n_used, csrc, cdst, w_gate, w_up, w_down, layer, t):
    nt = t // SORT_ROWS
    nb = _n_blocks(t)
    d = w_gate.shape[2]
    xs3 = xs.reshape(nt * TILE_CHUNKS, CHUNK, XS_W)
    n_out = nt * TILE_CHUNKS + _list_len(t)
    wmap = lambda b, be, nu, cs, cd: (layer, be[b], 0, 0)
    grid_spec = pltpu.PrefetchScalarGridSpec(
        num_scalar_prefetch=4,
        grid=(nb,),
        in_specs=[
            pl.BlockSpec(memory_space=pl.ANY),
            pl.BlockSpec((1, 1, d, D_EXPERT), wmap),
            pl.BlockSpec((1, 1, d, D_EXPERT), wmap),
            pl.BlockSpec((1, 1, D_EXPERT, d), wmap),
        ],
        out_specs=pl.BlockSpec(memory_space=pl.ANY),
        scratch_shapes=[
            pltpu.VMEM((2, BLOCK_CHUNKS, CHUNK, XS_W), U32),
            pltpu.VMEM((2, BLOCK_CHUNKS, CHUNK, PACK_W), U32),
            pltpu.VMEM((d, D_EXPERT), BF16),
            pltpu.VMEM((d, D_EXPERT), BF16),
            pltpu.VMEM((D_EXPERT, d), BF16),
            pltpu.SemaphoreType.DMA((2,)),
            pltpu.SemaphoreType.DMA((2,)),
        ],
    )
    return pl.pallas_call(
        _expert_kernel,
        grid_spec=grid_spec,
        out_shape=jax.ShapeDtypeStruct((n_out, CHUNK, PACK_W), U32),
        compiler_params=pltpu.CompilerParams(
            dimension_semantics=("arbitrary",), vmem_limit_bytes=VMEM_LIMIT),
        name="experts",
    )(blk_e, n_used, csrc, cdst, xs3, w_gate, w_up, w_down)


def _combine_kernel(x_ref, ys_ref, s_ref, meta_ref, g_ref, o_ref, *, final_norm):
    tl = x_ref.shape[0]
    cap = ys_ref.shape[0]
    used = meta_ref[0, 2:3, 0:1] * CHUNK
    yrow = lax.broadcasted_iota(I32, ys_ref.shape, 0)
    ys = ys_ref[...]
    live = yrow < used
    ylo = jnp.where(live, _unpack2(ys, 0), 0.0).astype(BF16)
    yhi = jnp.where(live, _unpack2(ys, 1), 0.0).astype(BF16)

    tr = lax.broadcasted_iota(I32, (tl, tl), 0)
    tc = lax.broadcasted_iota(I32, (tl, tl), 1)
    eye = tr == tc
    s = s_ref[0]
    s1 = jnp.sum(jnp.where(eye, s[0:1, :], 0), axis=1, keepdims=True)
    s2 = jnp.sum(jnp.where(eye, s[1:2, :], 0), axis=1, keepdims=True)
    col = lax.broadcasted_iota(I32, (tl, cap), 1)
    perm_t = ((col == s1) | (col == s2)).astype(BF16)
    half = ylo.shape[1]
    out = jnp.concatenate(
        [x_ref[:, :half] + jnp.dot(perm_t, ylo, preferred_element_type=F32),
         x_ref[:, half:] + jnp.dot(perm_t, yhi, preferred_element_type=F32)], axis=1)
    if final_norm:
        out = _rms(out, g_ref[...])
    o_ref[...] = out


def _combine(x2, ys, s12, meta, g_final, final_norm):
    t, d = x2.shape
    tl = SORT_ROWS
    nt = t // tl
    ys2 = ys.reshape(ys.shape[0] * CHUNK, PACK_W)
    return pl.pallas_call(
        functools.partial(_combine_kernel, final_norm=final_norm),
        grid=(nt,),
        in_specs=[
            pl.BlockSpec((tl, d), lambda i: (i, 0)),
            pl.BlockSpec((SORT_CAP, PACK_W), lambda i: (i, 0)),
            pl.BlockSpec((1, SUBLANES, tl), lambda i: (i, 0, 0)),
            pl.BlockSpec((1, SUBLANES, LANES), lambda i: (i, 0, 0)),
            pl.BlockSpec((1, d), lambda i: (0, 0)),
        ],
        out_specs=pl.BlockSpec((tl, d), lambda i: (i, 0)),
        out_shape=jax.ShapeDtypeStruct((t, d), F32),
        compiler_params=pltpu.CompilerParams(
            dimension_semantics=("parallel",), vmem_limit_bytes=VMEM_LIMIT),
        name="combine",
    )(x2, ys2, s12, meta, g_final)


def _rope_tables(seq):
    pos = jnp.arange(seq, dtype=F32)
    inv_freq = ROPE_THETA ** (-jnp.arange(0, ROPE_DIM, 2, dtype=F32) / ROPE_DIM)
    ang = pos[:, None] * inv_freq[None, :]
    cos, sin = jnp.cos(ang), jnp.sin(ang)
    half = ROPE_DIM // 2
    ones = jnp.ones((seq, HEAD_DIM - ROPE_DIM), F32)
    zeros_h = jnp.zeros((seq, half), F32)
    zeros_r = jnp.zeros((seq, HEAD_DIM - ROPE_DIM), F32)
    cos_h = jnp.concatenate([cos, cos, ones], axis=1)
    sp_h = jnp.concatenate([zeros_h, sin, zeros_r], axis=1)
    sm_h = jnp.concatenate([-sin, zeros_h, zeros_r], axis=1)
    rep = LANES // HEAD_DIM
    return (jnp.tile(cos_h, (1, rep)), jnp.tile(sp_h, (1, rep)), jnp.tile(sm_h, (1, rep)))


def kernel(x, norm_mix, w_in, attn_sink, conv_w, norm_attn_out, norm_conv_out, w_out, norm_ffn,
           w_router_group, b_router_group, w_router_expert, b_router_expert,
           w_expert_gate, w_expert_up, w_expert_down, norm_final):
    batch, seq, d = x.shape
    depth = w_in.shape[0]
    t = batch * seq
    assert seq % MIX_ROWS == 0 and seq % FRONT_ROWS == 0 and t % SORT_ROWS == 0
    cos_t, sp_t, sm_t = _rope_tables(seq)
    x2 = x.reshape(t, d)
    w_in_b = w_in.astype(BF16)
    w_out_b = w_out.astype(BF16)
    w_router, b_router = _router_params(w_router_group, b_router_group,
                                        w_router_expert, b_router_expert)
    g_final = norm_final.reshape(1, d)

    for l in range(depth):
        q, k, v, gb, cu = _front(x2, norm_mix[l].reshape(1, d), w_in_b[l], cos_t, sp_t, sm_t, seq)
        x2 = _mix(x2, q, k, v, gb, cu, attn_sink[l], conv_w[l],
                  norm_attn_out[l].reshape(1, -1), norm_conv_out[l].reshape(1, -1),
                  w_out_b[l], batch, seq)
        xs, s12, meta = _route(x2, norm_ffn[l].reshape(1, d), w_router[l], b_router[l])
        blk_e, n_used, csrc, cdst = _plan(meta, t)
        ys = _experts(xs, blk_e, n_used, csrc, cdst,
                      w_expert_gate, w_expert_up, w_expert_down, l, t)
        x2 = _combine(x2, ys, s12, meta, g_final, final_norm=(l == depth - 1))
    return x2.reshape(batch, seq, d)
```

```python
import functools

import jax
import jax.numpy as jnp
from jax import lax
from jax.experimental import pallas as pl
from jax.experimental.pallas import tpu as pltpu

F32 = jnp.float32
BF16 = jnp.bfloat16
U32 = jnp.uint32
I32 = jnp.int32

N_HEADS = 8
N_KV_HEADS = 2
HEAD_DIM = 64
ATTN_WIDTH = N_HEADS * HEAD_DIM
KV_WIDTH = N_KV_HEADS * HEAD_DIM
WINDOW = 128
QBLOCK = 128
ROPE_DIM = HEAD_DIM // 4
ROPE_THETA = 500000.0
CONV_WIDTH = 512
N_GROUPS = 4
PER_GROUP = 8
N_EXPERTS = N_GROUPS * PER_GROUP
D_EXPERT = 512
EPS = 1e-6
NEG_INF = -1e30

LANES = 128
SUBLANES = 8

FRONT_ROWS = 512
MIX_ROWS = 512
SORT_ROWS = 256
TILES_PER_STEP = 2
CHUNK = SUBLANES
BLOCK_ROWS = 512
BLOCK_CHUNKS = BLOCK_ROWS // CHUNK
SORT_CAP = ((2 * SORT_ROWS + N_EXPERTS * (CHUNK - 1)) // LANES + 1) * LANES
TILE_CHUNKS = SORT_CAP // CHUNK
PACK_W = 512
XS_W = PACK_W + LANES

VMEM_LIMIT = 56 * 1024 * 1024


def _rms(x, g):
    return x * lax.rsqrt(jnp.mean(x * x, axis=-1, keepdims=True) + EPS) * g


def _pack2(lo, hi):
    return pltpu.pack_elementwise([lo, hi], packed_dtype=BF16)


def _unpack2(w, index):
    return pltpu.unpack_elementwise(w, index=index, packed_dtype=BF16, unpacked_dtype=F32)


def _front_kernel(x_ref, g_ref, w_ref, cos_ref, sp_ref, sm_ref,
                  q_ref, k_ref, v_ref, b_ref, cu_ref):
    h = _rms(x_ref[...], g_ref[...]).astype(BF16)
    cos = cos_ref[...]
    sp = sp_ref[...]
    sm = sm_ref[...]

    def proj(lo, hi):
        return jnp.dot(h, w_ref[:, lo:hi], preferred_element_type=F32)

    def rope(t):
        half = ROPE_DIM // 2
        return (t * cos + pltpu.roll(t, half, axis=1) * sp
                + pltpu.roll(t, LANES - half, axis=1) * sm)

    wide = 2 * LANES
    for c in range(ATTN_WIDTH // wide):
        q = proj(c * wide, (c + 1) * wide)
        for j in range(2):
            q_ref[:, c * wide + j * LANES:c * wide + (j + 1) * LANES] = (
                rope(q[:, j * LANES:(j + 1) * LANES]) * (HEAD_DIM ** -0.5)).astype(BF16)
    o = ATTN_WIDTH
    kv = proj(o, o + 2 * KV_WIDTH)
    k_ref[...] = rope(kv[:, :KV_WIDTH]).astype(BF16)
    v_ref[...] = kv[:, KV_WIDTH:].astype(BF16)
    o += 2 * KV_WIDTH
    b_ref[...] = proj(o, o + CONV_WIDTH).astype(BF16)
    o += CONV_WIDTH
    gc = proj(o, o + CONV_WIDTH)
    o += CONV_WIDTH
    u = proj(o, o + CONV_WIDTH)
    cu_ref[...] = (gc * u).astype(BF16)


def _front(x2, g, w_in_b, cos_t, sp_t, sm_t, seq):
    t, d = x2.shape
    tm = FRONT_ROWS
    nseq = seq // tm
    row = lambda i: (i, 0)
    pos = lambda i: (i % nseq, 0)
    fixed = lambda i: (0, 0)
    return pl.pallas_call(
        _front_kernel,
        grid=(t // tm,),
        in_specs=[
            pl.BlockSpec((tm, d), row),
            pl.BlockSpec((1, d), fixed),
            pl.BlockSpec(w_in_b.shape, fixed),
            pl.BlockSpec((tm, LANES), pos),
            pl.BlockSpec((tm, LANES), pos),
            pl.BlockSpec((tm, LANES), pos),
        ],
        out_specs=[
            pl.BlockSpec((tm, ATTN_WIDTH), row),
            pl.BlockSpec((tm, KV_WIDTH), row),
            pl.BlockSpec((tm, KV_WIDTH), row),
            pl.BlockSpec((tm, CONV_WIDTH), row),
            pl.BlockSpec((tm, CONV_WIDTH), row),
        ],
        out_shape=[
            jax.ShapeDtypeStruct((t, ATTN_WIDTH), BF16),
            jax.ShapeDtypeStruct((t, KV_WIDTH), BF16),
            jax.ShapeDtypeStruct((t, KV_WIDTH), BF16),
            jax.ShapeDtypeStruct((t, CONV_WIDTH), BF16),
            jax.ShapeDtypeStruct((t, CONV_WIDTH), BF16),
        ],
        compiler_params=pltpu.CompilerParams(
            dimension_semantics=("parallel",), vmem_limit_bytes=VMEM_LIMIT),
        name="front",
    )(x2, g, w_in_b, cos_t, sp_t, sm_t)


def _mix_kernel(sink_ref, x_ref, q_ref, kp_ref, kc_ref, kn_ref, vp_ref, vc_ref, vn_ref,
                b_ref, cu_ref, cup_ref, cun_ref, cw_ref, ga_ref, gc_ref, wo_ref, bias_ref,
                o_ref, ya_ref, kd_ref, vd_ref, *, seq):
    tq = x_ref.shape[0]
    i = pl.program_id(1)
    nseq = pl.num_programs(1)
    grp = N_HEADS // N_KV_HEADS
    rows = grp * QBLOCK
    band = 3 * QBLOCK

    kcat = jnp.concatenate([kp_ref[...], kc_ref[...], kn_ref[...]], axis=0)
    vcat = jnp.concatenate([vp_ref[...], vc_ref[...], vn_ref[...]], axis=0)
    for h in range(N_KV_HEADS):
        kd_ref[h] = jnp.concatenate([kcat[:, h * HEAD_DIM:(h + 1) * HEAD_DIM]] * 2, axis=1)
        vd_ref[h] = jnp.concatenate([vcat[:, h * HEAD_DIM:(h + 1) * HEAD_DIM]] * 2, axis=1)

    lane = lax.broadcasted_iota(I32, (QBLOCK, LANES), 1)
    low = lane < HEAD_DIM
    hrow = lax.broadcasted_iota(I32, (rows, 1), 0) // QBLOCK
    sinks = []
    for hk in range(N_KV_HEADS):
        sink = jnp.zeros((rows, 1), F32)
        for g in range(grp):
            sink = jnp.where(hrow == g, sink_ref[hk * grp + g], sink)
        sinks.append(sink)

    def qblock(jb, carry):
        r0 = pl.multiple_of(jb * QBLOCK, QBLOCK)
        pos0 = i * tq + r0
        bias_prev = bias_ref[jnp.where(pos0 == 0, 1, 0)]
        bias_next = bias_ref[jnp.where(pos0 + QBLOCK == seq, 3, 2)]
        qb = q_ref[pl.ds(r0, QBLOCK), :]
        for hk in range(N_KV_HEADS):
            parts = []
            for cc in range(2):
                c = 2 * hk + cc
                q2 = qb[:, c * LANES:(c + 1) * LANES]
                parts.append(jnp.where(low, q2, jnp.zeros_like(q2)))
                parts.append(jnp.where(low, jnp.zeros_like(q2), q2))
            qs = jnp.concatenate(parts, axis=0)
            sink = sinks[hk]
            s = lax.dot_general(qs, kd_ref[hk, pl.ds(r0, band), :], (((1,), (1,)), ((), ())),
                                preferred_element_type=F32)
            s = jnp.concatenate([s[:, :QBLOCK] + jnp.concatenate([bias_prev] * grp, axis=0),
                                 s[:, QBLOCK:2 * QBLOCK],
                                 s[:, 2 * QBLOCK:] + jnp.concatenate([bias_next] * grp, axis=0)],
                                axis=1)
            m = jnp.maximum(jnp.max(s, axis=1, keepdims=True), sink)
            p = jnp.exp(s - m)
            den = jnp.sum(p, axis=1, keepdims=True) + jnp.exp(sink - m)
            o = jnp.dot(p.astype(BF16), vd_ref[hk, pl.ds(r0, band), :],
                        preferred_element_type=F32)
            o = o * (1.0 / den)
            for cc in range(2):
                c = 2 * hk + cc
                a = o[(2 * cc) * QBLOCK:(2 * cc + 1) * QBLOCK]
                b = o[(2 * cc + 1) * QBLOCK:(2 * cc + 2) * QBLOCK]
                ya_ref[pl.ds(r0, QBLOCK), c * LANES:(c + 1) * LANES] = jnp.where(low, a, b)
        return carry

    lax.fori_loop(0, tq // QBLOCK, qblock, 0)

    cu = cu_ref[...].astype(F32)
    halo = cup_ref.shape[0]
    prev_row = jnp.where(i > 0, cup_ref[halo - 1:halo, :].astype(F32), 0.0)
    next_row = jnp.where(i < nseq - 1, cun_ref[0:1, :].astype(F32), 0.0)
    ridx = lax.broadcasted_iota(I32, cu.shape, 0)
    up = jnp.where(ridx == 0, prev_row, pltpu.roll(cu, 1, axis=0))
    dn = jnp.where(ridx == tq - 1, next_row, pltpu.roll(cu, tq - 1, axis=0))
    conv = up * cw_ref[0:1, :] + cu * cw_ref[1:2, :] + dn * cw_ref[2:3, :]
    yc = _rms(b_ref[...].astype(F32) * conv, gc_ref[...]).astype(BF16)
    ya = _rms(ya_ref[...], ga_ref[...]).astype(BF16)
    o_ref[...] = (x_ref[...]
                  + jnp.dot(ya, wo_ref[:ATTN_WIDTH, :], preferred_element_type=F32)
                  + jnp.dot(yc, wo_ref[ATTN_WIDTH:, :], preferred_element_type=F32))


def _mix(x2, q, k, v, gb, cu, sink, conv_w, g_attn, g_conv, w_out_b, batch, seq):
    t, d = x2.shape
    tq = MIX_ROWS
    nseq = seq // tq
    kb = tq // QBLOCK
    nkb = seq // QBLOCK
    halo = 16
    hb = tq // halo
    nhb = seq // halo

    row = lambda b, i: (b * nseq + i, 0)
    kprev = lambda b, i: (b * nkb + jnp.maximum(i * kb - 1, 0), 0)
    knext = lambda b, i: (b * nkb + jnp.minimum((i + 1) * kb, nkb - 1), 0)
    cprev = lambda b, i: (b * nhb + jnp.maximum(i * hb - 1, 0), 0)
    cnext = lambda b, i: (b * nhb + jnp.minimum((i + 1) * hb, nhb - 1), 0)
    fixed = lambda b, i: (0, 0)
    qi = lax.broadcasted_iota(I32, (QBLOCK, QBLOCK), 0)
    kj = lax.broadcasted_iota(I32, (QBLOCK, QBLOCK), 1)
    neg = jnp.full((QBLOCK, QBLOCK), NEG_INF, F32)
    band_bias = jnp.stack([jnp.where(kj - QBLOCK - qi >= -WINDOW, 0.0, NEG_INF).astype(F32), neg,
                           jnp.where(kj + QBLOCK - qi <= WINDOW, 0.0, NEG_INF).astype(F32), neg])
    return pl.pallas_call(
        functools.partial(_mix_kernel, seq=seq),
        grid=(batch, nseq),
        in_specs=[
            pl.BlockSpec(memory_space=pltpu.SMEM),
            pl.BlockSpec((tq, d), row),
            pl.BlockSpec((tq, ATTN_WIDTH), row),
            pl.BlockSpec((QBLOCK, KV_WIDTH), kprev),
            pl.BlockSpec((tq, KV_WIDTH), row),
            pl.BlockSpec((QBLOCK, KV_WIDTH), knext),
            pl.BlockSpec((QBLOCK, KV_WIDTH), kprev),
            pl.BlockSpec((tq, KV_WIDTH), row),
            pl.BlockSpec((QBLOCK, KV_WIDTH), knext),
            pl.BlockSpec((tq, CONV_WIDTH), row),
            pl.BlockSpec((tq, CONV_WIDTH), row),
            pl.BlockSpec((halo, CONV_WIDTH), cprev),
            pl.BlockSpec((halo, CONV_WIDTH), cnext),
            pl.BlockSpec(conv_w.shape, fixed),
            pl.BlockSpec((1, ATTN_WIDTH), fixed),
            pl.BlockSpec((1, CONV_WIDTH), fixed),
            pl.BlockSpec(w_out_b.shape, fixed),
            pl.BlockSpec(band_bias.shape, lambda b, i: (0, 0, 0)),
        ],
        out_specs=pl.BlockSpec((tq, d), row),
        out_shape=jax.ShapeDtypeStruct((t, d), F32),
        scratch_shapes=[pltpu.VMEM((tq, ATTN_WIDTH), F32),
                        pltpu.VMEM((N_KV_HEADS, tq + 2 * QBLOCK, LANES), BF16),
                        pltpu.VMEM((N_KV_HEADS, tq + 2 * QBLOCK, LANES), BF16)],
        compiler_params=pltpu.CompilerParams(
            dimension_semantics=("parallel", "parallel"), vmem_limit_bytes=VMEM_LIMIT),
        name="mix",
    )(sink, x2, q, k, k, k, v, v, v, gb, cu, cu, cu, conv_w, g_attn, g_conv, w_out_b, band_bias)


ROUTER_ROWS = N_EXPERTS + SUBLANES


def _route_kernel(x_ref, g_ref, w_ref, b_ref, xs_ref, s_ref, meta_ref):
    for u in range(xs_ref.shape[0]):
        _route_tile(u, x_ref, g_ref, w_ref, b_ref, xs_ref, s_ref, meta_ref)


def _route_tile(u, x_ref, g_ref, w_ref, b_ref, xs_ref, s_ref, meta_ref):
    tl = SORT_ROWS
    h = _rms(x_ref[u * tl:(u + 1) * tl, :], g_ref[...])
    hb = h.astype(BF16)

    dn = (((1,), (1,)), ((), ()))
    h2 = (h - hb.astype(F32)).astype(BF16)
    both = lax.dot_general(w_ref[...], hb, dn, preferred_element_type=F32)
    corr = lax.dot_general(w_ref[:ROUTER_ROWS, :], h2, dn, preferred_element_type=F32)
    logits = both[:ROUTER_ROWS] + both[ROUTER_ROWS:] + corr + b_ref[...]
    el = logits[:N_EXPERTS]
    gl = logits[N_EXPERTS:]

    r8 = lax.broadcasted_iota(I32, gl.shape, 0)
    glm = jnp.where(r8 < N_GROUPS, gl, -jnp.inf)
    gmax = jnp.max(glm, axis=0, keepdims=True)
    grp = jnp.min(jnp.where(glm == gmax, r8, SUBLANES), axis=0, keepdims=True)
    p_grp = 1.0 / jnp.sum(jnp.exp(glm - gmax), axis=0, keepdims=True)

    r32 = lax.broadcasted_iota(I32, el.shape, 0)
    e0 = jnp.where((r32 // PER_GROUP) == grp, el, -jnp.inf)
    t1 = jnp.max(e0, axis=0, keepdims=True)
    i1 = jnp.min(jnp.where(e0 == t1, r32, N_EXPERTS), axis=0, keepdims=True)
    a1 = r32 == i1
    e1 = jnp.where(a1, -jnp.inf, e0)
    t2 = jnp.max(e1, axis=0, keepdims=True)
    i2 = jnp.min(jnp.where(e1 == t2, r32, N_EXPERTS), axis=0, keepdims=True)
    a2 = r32 == i2
    ed = jnp.exp(t2 - t1)
    inv = 1.0 / (1.0 + ed)
    g1 = p_grp * inv
    g2 = p_grp * (ed * inv)

    af = (a1 | a2).astype(F32)
    cnt = jnp.sum(af, axis=1, keepdims=True)
    pc = jnp.floor((cnt + (CHUNK - 1)) * (1.0 / CHUNK)) * CHUNK
    er = lax.broadcasted_iota(I32, (N_EXPERTS, LANES), 0)
    ec = lax.broadcasted_iota(I32, (N_EXPERTS, LANES), 1)
    pc_row = jnp.sum(jnp.where(er == ec, pc, 0.0), axis=0, keepdims=True)
    lo_row = jnp.sum(jnp.where(er < ec, pc, 0.0), axis=0, keepdims=True)
    lo_col = jnp.sum(jnp.where(ec < er, pc_row, 0.0), axis=1, keepdims=True)
    total = jnp.sum(pc, axis=0, keepdims=True)

    tr = lax.broadcasted_iota(I32, (tl, tl), 0)
    tc = lax.broadcasted_iota(I32, (tl, tl), 1)
    upper = (tr <= tc).astype(BF16)
    incl = jnp.dot(af.astype(BF16), upper, preferred_element_type=F32)
    slot = lo_col + incl - af
    s1 = jnp.sum(jnp.where(a1, slot, 0.0), axis=0, keepdims=True).astype(I32)
    s2 = jnp.sum(jnp.where(a2, slot, 0.0), axis=0, keepdims=True).astype(I32)

    cap = xs_ref.shape[1]
    srow = lax.broadcasted_iota(I32, (cap, tl), 0)
    p1 = srow == s1
    p2 = srow == s2
    perm = (p1 | p2).astype(BF16)
    half = hb.shape[1] // 2
    lo = jnp.dot(perm, hb[:, :half], preferred_element_type=F32)
    hi = jnp.dot(perm, hb[:, half:], preferred_element_type=F32)
    xs_ref[u, :, :PACK_W] = _pack2(lo, hi)
    gs = jnp.sum(jnp.where(p1, g1, 0.0) + jnp.where(p2, g2, 0.0), axis=1, keepdims=True)
    gs = jnp.broadcast_to(gs, (cap, LANES))
    gs_hi = gs.astype(BF16).astype(F32)
    xs_ref[u, :, PACK_W:] = _pack2(gs_hi, gs - gs_hi)

    sr = lax.broadcasted_iota(I32, (SUBLANES, tl), 0)
    s_ref[u] = jnp.where(sr == 0, s1, jnp.where(sr == 1, s2, 0))
    mr = lax.broadcasted_iota(I32, (SUBLANES, LANES), 0)
    meta = jnp.where(mr == 0, pc_row, jnp.where(mr == 1, lo_row, jnp.where(mr == 2, total, 0.0)))
    meta_ref[u] = (meta * (1.0 / CHUNK)).astype(I32)


def _router_params(w_group, b_group, w_expert, b_expert):
    pad = SUBLANES - N_GROUPS
    w = jnp.concatenate([jnp.swapaxes(w_expert, 1, 2),
                         jnp.pad(jnp.swapaxes(w_group, 1, 2), ((0, 0), (0, pad), (0, 0)))], axis=1)
    w1 = w.astype(BF16)
    w2 = (w - w1.astype(F32)).astype(BF16)
    b = jnp.concatenate([b_expert, jnp.pad(b_group, ((0, 0), (0, pad)))], axis=1)[:, :, None]
    return jnp.concatenate([w1, w2], axis=1), b


def _route(x2, g, w_router, b_router):
    t, d = x2.shape
    tl = SORT_ROWS
    nt = t // tl
    fixed = lambda i: (0, 0)
    per = TILES_PER_STEP
    return pl.pallas_call(
        _route_kernel,
        grid=(nt // per,),
        in_specs=[
            pl.BlockSpec((per * tl, d), lambda i: (i, 0)),
            pl.BlockSpec((1, d), fixed),
            pl.BlockSpec(w_router.shape, fixed),
            pl.BlockSpec(b_router.shape, fixed),
        ],
        out_specs=[
            pl.BlockSpec((per, SORT_CAP, XS_W), lambda i: (i, 0, 0)),
            pl.BlockSpec((per, SUBLANES, tl), lambda i: (i, 0, 0)),
            pl.BlockSpec((per, SUBLANES, LANES), lambda i: (i, 0, 0)),
        ],
        out_shape=[
            jax.ShapeDtypeStruct((nt, SORT_CAP, XS_W), U32),
            jax.ShapeDtypeStruct((nt, SUBLANES, tl), I32),
            jax.ShapeDtypeStruct((nt, SUBLANES, LANES), I32),
        ],
        compiler_params=pltpu.CompilerParams(
            dimension_semantics=("parallel",), vmem_limit_bytes=VMEM_LIMIT),
        name="route",
    )(x2, g, w_router, b_router)


def _n_blocks(t):
    nt = t // SORT_ROWS
    max_chunks = (2 * t + (CHUNK - 1) * N_EXPERTS * nt) // CHUNK + (BLOCK_CHUNKS - 1) * N_EXPERTS
    return -(-max_chunks // BLOCK_CHUNKS)


PLAN_LANES = 1024


def _list_len(t):
    return -(-(_n_blocks(t) + 1) * BLOCK_CHUNKS // PLAN_LANES) * PLAN_LANES


def _plan_kernel(nch_ref, ncht_ref, lo8_ref, csrc_ref, cdst_ref, blk_ref, nu_ref):
    nt = nch_ref.shape[0]
    ch = csrc_ref.shape[1]
    nch = nch_ref[:, :N_EXPERTS].astype(F32)
    lo8 = lo8_ref[:, :N_EXPERTS].astype(F32)
    ncht = ncht_ref[...].astype(F32)

    tc = jnp.sum(ncht, axis=1, keepdims=True)
    ptc = jnp.floor((tc + (BLOCK_CHUNKS - 1)) * (1.0 / BLOCK_CHUNKS)) * BLOCK_CHUNKS
    er = lax.broadcasted_iota(I32, (N_EXPERTS, LANES), 0)
    ec = lax.broadcasted_iota(I32, (N_EXPERTS, LANES), 1)
    ptc_row = jnp.sum(jnp.where(er == ec, ptc, 0.0), axis=0, keepdims=True)
    eend = jnp.sum(jnp.where(ec <= er, ptc_row, 0.0), axis=1, keepdims=True)
    estart = eend - ptc
    total = jnp.sum(ptc, axis=0, keepdims=True)

    tr = lax.broadcasted_iota(I32, (nt, nt), 0)
    tcol = lax.broadcasted_iota(I32, (nt, nt), 1)
    lower = (tcol <= tr).astype(BF16)
    cum = jnp.dot(lower, nch.astype(BF16), preferred_element_type=F32)
    cum_hi = jnp.floor(cum * (1.0 / LANES))
    cum_lo = cum - cum_hi * LANES

    k = (pl.program_id(0) * ch + lax.broadcasted_iota(I32, (1, ch), 1)).astype(F32)
    r32 = lax.broadcasted_iota(I32, (N_EXPERTS, ch), 0).astype(F32)

    def expert_of(pos):
        return jnp.minimum(jnp.sum((eend <= pos).astype(F32), axis=0, keepdims=True),
                           N_EXPERTS - 1.0)

    e_k = expert_of(k)
    sel_e = r32 == e_k
    oh_e = sel_e.astype(BF16)
    start_k = jnp.sum(jnp.where(sel_e, estart, 0.0), axis=0, keepdims=True)
    tc_k = jnp.sum(jnp.where(sel_e, tc, 0.0), axis=0, keepdims=True)
    q = k - start_k

    def pick(tbl):
        return jnp.dot(tbl.astype(BF16), oh_e, preferred_element_type=F32)

    cum_e = pick(cum_hi) * LANES + pick(cum_lo)
    nch_e = pick(nch)
    lo8_e = pick(lo8)
    rnt = lax.broadcasted_iota(I32, (nt, ch), 0).astype(F32)
    i_k = jnp.minimum(jnp.sum((cum_e <= q).astype(F32), axis=0, keepdims=True), nt - 1.0)
    sel_i = rnt == i_k
    before = jnp.sum(jnp.where(sel_i, cum_e - nch_e, 0.0), axis=0, keepdims=True)
    lo8_k = jnp.sum(jnp.where(sel_i, lo8_e, 0.0), axis=0, keepdims=True)
    src = i_k * TILE_CHUNKS + lo8_k + (q - before)
    pad = (q >= tc_k) | (k >= total)
    csrc_ref[...] = jnp.where(pad, TILE_CHUNKS - 1.0, src).astype(I32)
    cdst_ref[...] = jnp.where(pad, nt * TILE_CHUNKS + k, src).astype(I32)

    n_used = total * (1.0 / BLOCK_CHUNKS)
    bpos = lax.broadcasted_iota(I32, blk_ref.shape, 1).astype(F32)
    blk_ref[...] = expert_of(jnp.minimum(bpos, n_used - 1.0) * BLOCK_CHUNKS).astype(I32)
    nu_ref[...] = jnp.broadcast_to(n_used, nu_ref.shape).astype(I32)


def _plan(meta, t):
    nt = t // SORT_ROWS
    nl = _list_len(t)
    nbp = -(-_n_blocks(t) // LANES) * LANES
    nch = meta[:, 0, :]
    lo8 = meta[:, 1, :]
    ncht = nch[:, :N_EXPERTS].T
    fixed = lambda c: (0, 0)
    csrc, cdst, blk_e, n_used = pl.pallas_call(
        _plan_kernel,
        grid=(nl // PLAN_LANES,),
        in_specs=[
            pl.BlockSpec(nch.shape, fixed),
            pl.BlockSpec(ncht.shape, fixed),
            pl.BlockSpec(lo8.shape, fixed),
        ],
        out_specs=[
            pl.BlockSpec((1, PLAN_LANES), lambda c: (0, c)),
            pl.BlockSpec((1, PLAN_LANES), lambda c: (0, c)),
            pl.BlockSpec((1, nbp), fixed),
            pl.BlockSpec((1, LANES), fixed),
        ],
        out_shape=[
            jax.ShapeDtypeStruct((1, nl), I32),
            jax.ShapeDtypeStruct((1, nl), I32),
            jax.ShapeDtypeStruct((1, nbp), I32),
            jax.ShapeDtypeStruct((1, LANES), I32),
        ],
        compiler_params=pltpu.CompilerParams(dimension_semantics=("arbitrary",)),
        name="plan",
    )(nch, ncht, lo8)
    return blk_e.reshape(-1), n_used.reshape(-1), csrc.reshape(-1), cdst.reshape(-1)


def _expert_kernel(be_ref, nu_ref, csrc_ref, cdst_ref,
                   xs_hbm, wg_ref, wu_ref, wd_ref, ys_hbm,
                   xbuf, ybuf, wgb, wub, wdb, sem_in, sem_out):
    b = pl.program_id(0)
    n_used = nu_ref[0]
    slot = b % 2

    def gather(blk, sl):
        return [pltpu.make_async_copy(xs_hbm.at[csrc_ref[blk * BLOCK_CHUNKS + j]],
                                      xbuf.at[sl, j], sem_in.at[sl])
                for j in range(BLOCK_CHUNKS)]

    def scatter(blk, sl):
        return [pltpu.make_async_copy(ybuf.at[sl, j],
                                      ys_hbm.at[cdst_ref[blk * BLOCK_CHUNKS + j]], sem_out.at[sl])
                for j in range(BLOCK_CHUNKS)]

    @pl.when(b == 0)
    def _():
        for c in gather(0, 0):
            c.start()

    new_expert = jnp.logical_or(b == 0, be_ref[b] != be_ref[jnp.maximum(b - 1, 0)])

    @pl.when(jnp.logical_and(b < n_used, new_expert))
    def _():
        wgb[...] = wg_ref[0, 0].astype(BF16)
        wub[...] = wu_ref[0, 0].astype(BF16)
        wdb[...] = wd_ref[0, 0].astype(BF16)

    @pl.when(b < n_used)
    def _():
        for c in gather(b, slot):
            c.wait()
        for c in gather(b + 1, 1 - slot):
            c.start()
        w = xbuf[slot].reshape(BLOCK_ROWS, XS_W)
        xw = w[:, :PACK_W]
        lo = _unpack2(xw, 0).astype(BF16)
        hi = _unpack2(xw, 1).astype(BF16)
        gw = w[:, PACK_W:]
        gate = (_unpack2(gw, 0) + _unpack2(gw, 1))[:, 0:1]
        half = PACK_W
        hg = (jnp.dot(lo, wgb[:half, :], preferred_element_type=F32)
              + jnp.dot(hi, wgb[half:, :], preferred_element_type=F32))
        hu = (jnp.dot(lo, wub[:half, :], preferred_element_type=F32)
              + jnp.dot(hi, wub[half:, :], preferred_element_type=F32))
        act = (hg * (1.0 / (1.0 + jnp.exp(-hg))) * hu).astype(BF16)
        y = jnp.dot(act, wdb[...], preferred_element_type=F32) * gate
        ybuf[slot] = _pack2(y[:, :half], y[:, half:]).reshape(BLOCK_CHUNKS, CHUNK, PACK_W)
        for c in scatter(b, slot):
            c.start()

    @pl.when(jnp.logical_and(b >= 1, b < n_used))
    def _():
        for c in scatter(b - 1, 1 - slot):
            c.wait()

    @pl.when(b == n_used - 1)
    def _():
        for c in scatter(b, slot):
            c.wait()
        for c in gather(b + 1, 1 - slot):
            c.wait()


def _experts(xs, blk_e, n_used, csrc, cdst, w_gate, w_up, w_down, layer, t):
    nt = t // SORT_ROWS
    nb = _n_blocks(t)
    d = w_gate.shape[2]
    xs3 = xs.reshape(nt * TILE_CHUNKS, CHUNK, XS_W)
    n_out = nt * TILE_CHUNKS + _list_len(t)
    wmap = lambda b, be, nu, cs, cd: (layer, be[b], 0, 0)
    grid_spec = pltpu.PrefetchScalarGridSpec(
        num_scalar_prefetch=4,
        grid=(nb,),
        in_specs=[
            pl.BlockSpec(memory_space=pl.ANY),
            pl.BlockSpec((1, 1, d, D_EXPERT), wmap),
            pl.BlockSpec((1, 1, d, D_EXPERT), wmap),
            pl.BlockSpec((1, 1, D_EXPERT, d), wmap),
        ],
        out_specs=pl.BlockSpec(memory_space=pl.ANY),
        scratch_shapes=[
            pltpu.VMEM((2, BLOCK_CHUNKS, CHUNK, XS_W), U32),
            pltpu.VMEM((2, BLOCK_CHUNKS, CHUNK, PACK_W), U32),
            pltpu.VMEM((d, D_EXPERT), BF16),
            pltpu.VMEM((d, D_EXPERT), BF16),
            pltpu.VMEM((D_EXPERT, d), BF16),
            pltpu.SemaphoreType.DMA((2,)),
            pltpu.SemaphoreType.DMA((2,)),
        ],
    )
    return pl.pallas_call(
        _expert_kernel,
        grid_spec=grid_spec,
        out_shape=jax.ShapeDtypeStruct((n_out, CHUNK, PACK_W), U32),
        compiler_params=pltpu.CompilerParams(
            dimension_semantics=("arbitrary",), vmem_limit_bytes=VMEM_LIMIT),
        name="experts",
    )(blk_e, n_used, csrc, cdst, xs3, w_gate, w_up, w_down)


def _combine_kernel(x_ref, ys_ref, s_ref, meta_ref, g_ref, o_ref, *, final_norm):
    tl = SORT_ROWS
    cap = SORT_CAP
    tr = lax.broadcasted_iota(I32, (tl, tl), 0)
    tc = lax.broadcasted_iota(I32, (tl, tl), 1)
    eye = tr == tc
    col = lax.broadcasted_iota(I32, (tl, cap), 1)
    yrow = lax.broadcasted_iota(I32, (cap, PACK_W), 0)
    for u in range(s_ref.shape[0]):
        used = meta_ref[u, 2:3, 0:1] * CHUNK
        ys = ys_ref[u * cap:(u + 1) * cap, :]
        live = yrow < used
        ylo = jnp.where(live, _unpack2(ys, 0), 0.0).astype(BF16)
        yhi = jnp.where(live, _unpack2(ys, 1), 0.0).astype(BF16)
        s = s_ref[u]
        s1 = jnp.sum(jnp.where(eye, s[0:1, :], 0), axis=1, keepdims=True)
        s2 = jnp.sum(jnp.where(eye, s[1:2, :], 0), axis=1, keepdims=True)
        perm_t = ((col == s1) | (col == s2)).astype(BF16)
        half = PACK_W
        rows = slice(u * tl, (u + 1) * tl)
        out = jnp.concatenate(
            [x_ref[rows, :half] + jnp.dot(perm_t, ylo, preferred_element_type=F32),
             x_ref[rows, half:] + jnp.dot(perm_t, yhi, preferred_element_type=F32)], axis=1)
        if final_norm:
            out = _rms(out, g_ref[...])
        o_ref[rows, :] = out


def _combine(x2, ys, s12, meta, g_final, final_norm):
    t, d = x2.shape
    tl = SORT_ROWS
    nt = t // tl
    ys2 = ys.reshape(ys.shape[0] * CHUNK, PACK_W)
    per = TILES_PER_STEP
    return pl.pallas_call(
        functools.partial(_combine_kernel, final_norm=final_norm),
        grid=(nt // per,),
        in_specs=[
            pl.BlockSpec((per * tl, d), lambda i: (i, 0)),
            pl.BlockSpec((per * SORT_CAP, PACK_W), lambda i: (i, 0)),
            pl.BlockSpec((per, SUBLANES, tl), lambda i: (i, 0, 0)),
            pl.BlockSpec((per, SUBLANES, LANES), lambda i: (i, 0, 0)),
            pl.BlockSpec((1, d), lambda i: (0, 0)),
        ],
        out_specs=pl.BlockSpec((per * tl, d), lambda i: (i, 0)),
        out_shape=jax.ShapeDtypeStruct((t, d), F32),
        compiler_params=pltpu.CompilerParams(
            dimension_semantics=("parallel",), vmem_limit_bytes=VMEM_LIMIT),
        name="combine",
    )(x2, ys2, s12, meta, g_final)


def _rope_tables(seq):
    pos = jnp.arange(seq, dtype=F32)
    inv_freq = ROPE_THETA ** (-jnp.arange(0, ROPE_DIM, 2, dtype=F32) / ROPE_DIM)
    ang = pos[:, None] * inv_freq[None, :]
    cos, sin = jnp.cos(ang), jnp.sin(ang)
    half = ROPE_DIM // 2
    ones = jnp.ones((seq, HEAD_DIM - ROPE_DIM), F32)
    zeros_h = jnp.zeros((seq, half), F32)
    zeros_r = jnp.zeros((seq, HEAD_DIM - ROPE_DIM), F32)
    cos_h = jnp.concatenate([cos, cos, ones], axis=1)
    sp_h = jnp.concatenate([zeros_h, sin, zeros_r], axis=1)
    sm_h = jnp.concatenate([-sin, zeros_h, zeros_r], axis=1)
    rep = LANES // HEAD_DIM
    return (jnp.tile(cos_h, (1, rep)), jnp.tile(sp_h, (1, rep)), jnp.tile(sm_h, (1, rep)))


def kernel(x, norm_mix, w_in, attn_sink, conv_w, norm_attn_out, norm_conv_out, w_out, norm_ffn,
           w_router_group, b_router_group, w_router_expert, b_router_expert,
           w_expert_gate, w_expert_up, w_expert_down, norm_final):
    batch, seq, d = x.shape
    depth = w_in.shape[0]
    t = batch * seq
    assert seq % MIX_ROWS == 0 and seq % FRONT_ROWS == 0 and t % SORT_ROWS == 0
    cos_t, sp_t, sm_t = _rope_tables(seq)
    x2 = x.reshape(t, d)
    w_in_b = w_in.astype(BF16)
    w_out_b = w_out.astype(BF16)
    w_router, b_router = _router_params(w_router_group, b_router_group,
                                        w_router_expert, b_router_expert)
    g_final = norm_final.reshape(1, d)

    for l in range(depth):
        q, k, v, gb, cu = _front(x2, norm_mix[l].reshape(1, d), w_in_b[l], cos_t, sp_t, sm_t, seq)
        x2 = _mix(x2, q, k, v, gb, cu, attn_sink[l], conv_w[l],
                  norm_attn_out[l].reshape(1, -1), norm_conv_out[l].reshape(1, -1),
                  w_out_b[l], batch, seq)
        xs, s12, meta = _route(x2, norm_ffn[l].reshape(1, d), w_router[l], b_router[l])
        blk_e, n_used, csrc, cdst = _plan(meta, t)
        ys = _experts(xs, blk_e, n_used, csrc, cdst,
                      w_expert_gate, w_expert_up, w_expert_down, l, t)
        x2 = _combine(x2, ys, s12, meta, g_final, final_norm=(l == depth - 1))
    return x2.reshape(batch, seq, d)
```

```python
import functools

import jax
import jax.numpy as jnp
from jax import lax
from jax.experimental import pallas as pl
from jax.experimental.pallas import tpu as pltpu

F32 = jnp.float32
BF16 = jnp.bfloat16
U32 = jnp.uint32
I32 = jnp.int32

N_HEADS = 8
N_KV_HEADS = 2
HEAD_DIM = 64
ATTN_WIDTH = N_HEADS * HEAD_DIM
KV_WIDTH = N_KV_HEADS * HEAD_DIM
WINDOW = 128
QBLOCK = 128
ROPE_DIM = HEAD_DIM // 4
ROPE_THETA = 500000.0
CONV_WIDTH = 512
N_GROUPS = 4
PER_GROUP = 8
N_EXPERTS = N_GROUPS * PER_GROUP
D_EXPERT = 512
EPS = 1e-6
NEG_INF = -1e30

LANES = 128
SUBLANES = 8

FRONT_ROWS = 512
MIX_ROWS = 512
SORT_ROWS = 256
TILES_PER_STEP = 2
CHUNK = SUBLANES
BLOCK_ROWS = 512
BLOCK_CHUNKS = BLOCK_ROWS // CHUNK
SORT_CAP = ((2 * SORT_ROWS + N_EXPERTS * (CHUNK - 1)) // LANES + 1) * LANES
TILE_CHUNKS = SORT_CAP // CHUNK
PACK_W = 512
XS_W = PACK_W + LANES

VMEM_LIMIT = 56 * 1024 * 1024


def _rms(x, g):
    return x * lax.rsqrt(jnp.mean(x * x, axis=-1, keepdims=True) + EPS) * g


def _pack2(lo, hi):
    return pltpu.pack_elementwise([lo, hi], packed_dtype=BF16)


def _unpack2(w, index):
    return pltpu.unpack_elementwise(w, index=index, packed_dtype=BF16, unpacked_dtype=F32)


def _combine_front_kernel(x_ref, ys_ref, s_ref, meta_ref, g_ref, w_ref, cos_ref, sp_ref, sm_ref,
                          xo_ref, q_ref, k_ref, v_ref, b_ref, cu_ref):
    _combine_tiles(x_ref, ys_ref, s_ref, meta_ref, xo_ref, None)
    _front_kernel(xo_ref, g_ref, w_ref, cos_ref, sp_ref, sm_ref, q_ref, k_ref, v_ref, b_ref, cu_ref)


def _front_kernel(x_ref, g_ref, w_ref, cos_ref, sp_ref, sm_ref,
                  q_ref, k_ref, v_ref, b_ref, cu_ref):
    h = _rms(x_ref[...], g_ref[...]).astype(BF16)
    cos = cos_ref[...]
    sp = sp_ref[...]
    sm = sm_ref[...]

    def proj(lo, hi):
        return jnp.dot(h, w_ref[:, lo:hi], preferred_element_type=F32)

    def rope(t):
        half = ROPE_DIM // 2
        return (t * cos + pltpu.roll(t, half, axis=1) * sp
                + pltpu.roll(t, LANES - half, axis=1) * sm)

    wide = 2 * LANES
    for c in range(ATTN_WIDTH // wide):
        q = proj(c * wide, (c + 1) * wide)
        for j in range(2):
            q_ref[:, c * wide + j * LANES:c * wide + (j + 1) * LANES] = (
                rope(q[:, j * LANES:(j + 1) * LANES]) * (HEAD_DIM ** -0.5)).astype(BF16)
    o = ATTN_WIDTH
    kv = proj(o, o + 2 * KV_WIDTH)
    k_ref[...] = rope(kv[:, :KV_WIDTH]).astype(BF16)
    v_ref[...] = kv[:, KV_WIDTH:].astype(BF16)
    o += 2 * KV_WIDTH
    b_ref[...] = proj(o, o + CONV_WIDTH).astype(BF16)
    o += CONV_WIDTH
    gc = proj(o, o + CONV_WIDTH)
    o += CONV_WIDTH
    u = proj(o, o + CONV_WIDTH)
    cu_ref[...] = (gc * u).astype(BF16)


def _combine_front(x2, ys, s12, meta, g, w_in_b, cos_t, sp_t, sm_t, seq):
    t, d = x2.shape
    tm = FRONT_ROWS
    per = tm // SORT_ROWS
    nseq = seq // tm
    row = lambda i: (i, 0)
    tile = lambda i: (i, 0, 0)
    pos = lambda i: (i % nseq, 0)
    fixed = lambda i: (0, 0)
    ys2 = ys.reshape(ys.shape[0] * CHUNK, PACK_W)
    return pl.pallas_call(
        _combine_front_kernel,
        grid=(t // tm,),
        in_specs=[
            pl.BlockSpec((tm, d), row),
            pl.BlockSpec((per * SORT_CAP, PACK_W), row),
            pl.BlockSpec((per, SUBLANES, SORT_ROWS), tile),
            pl.BlockSpec((per, SUBLANES, LANES), tile),
            pl.BlockSpec((1, d), fixed),
            pl.BlockSpec(w_in_b.shape, fixed),
            pl.BlockSpec((tm, LANES), pos),
            pl.BlockSpec((tm, LANES), pos),
            pl.BlockSpec((tm, LANES), pos),
        ],
        out_specs=[
            pl.BlockSpec((tm, d), row),
            pl.BlockSpec((tm, ATTN_WIDTH), row),
            pl.BlockSpec((tm, KV_WIDTH), row),
            pl.BlockSpec((tm, KV_WIDTH), row),
            pl.BlockSpec((tm, CONV_WIDTH), row),
            pl.BlockSpec((tm, CONV_WIDTH), row),
        ],
        out_shape=[
            jax.ShapeDtypeStruct((t, d), F32),
            jax.ShapeDtypeStruct((t, ATTN_WIDTH), BF16),
            jax.ShapeDtypeStruct((t, KV_WIDTH), BF16),
            jax.ShapeDtypeStruct((t, KV_WIDTH), BF16),
            jax.ShapeDtypeStruct((t, CONV_WIDTH), BF16),
            jax.ShapeDtypeStruct((t, CONV_WIDTH), BF16),
        ],
        compiler_params=pltpu.CompilerParams(
            dimension_semantics=("parallel",), vmem_limit_bytes=VMEM_LIMIT),
        name="combine_front",
    )(x2, ys2, s12, meta, g, w_in_b, cos_t, sp_t, sm_t)


def _front(x2, g, w_in_b, cos_t, sp_t, sm_t, seq):
    t, d = x2.shape
    tm = FRONT_ROWS
    nseq = seq // tm
    row = lambda i: (i, 0)
    pos = lambda i: (i % nseq, 0)
    fixed = lambda i: (0, 0)
    return pl.pallas_call(
        _front_kernel,
        grid=(t // tm,),
        in_specs=[
            pl.BlockSpec((tm, d), row),
            pl.BlockSpec((1, d), fixed),
            pl.BlockSpec(w_in_b.shape, fixed),
            pl.BlockSpec((tm, LANES), pos),
            pl.BlockSpec((tm, LANES), pos),
            pl.BlockSpec((tm, LANES), pos),
        ],
        out_specs=[
            pl.BlockSpec((tm, ATTN_WIDTH), row),
            pl.BlockSpec((tm, KV_WIDTH), row),
            pl.BlockSpec((tm, KV_WIDTH), row),
            pl.BlockSpec((tm, CONV_WIDTH), row),
            pl.BlockSpec((tm, CONV_WIDTH), row),
        ],
        out_shape=[
            jax.ShapeDtypeStruct((t, ATTN_WIDTH), BF16),
            jax.ShapeDtypeStruct((t, KV_WIDTH), BF16),
            jax.ShapeDtypeStruct((t, KV_WIDTH), BF16),
            jax.ShapeDtypeStruct((t, CONV_WIDTH), BF16),
            jax.ShapeDtypeStruct((t, CONV_WIDTH), BF16),
        ],
        compiler_params=pltpu.CompilerParams(
            dimension_semantics=("parallel",), vmem_limit_bytes=VMEM_LIMIT),
        name="front",
    )(x2, g, w_in_b, cos_t, sp_t, sm_t)


def _mix_kernel(sink_ref, x_ref, q_ref, kp_ref, kc_ref, kn_ref, vp_ref, vc_ref, vn_ref,
                b_ref, cu_ref, cup_ref, cun_ref, cw_ref, ga_ref, gc_ref, wo_ref, bias_ref,
                gf_ref, wr_ref, br_ref,
                o_ref, xs_ref, s_ref, meta_ref, ya_ref, kd_ref, vd_ref, *, seq):
    tq = x_ref.shape[0]
    i = pl.program_id(1)
    nseq = pl.num_programs(1)
    grp = N_HEADS // N_KV_HEADS
    rows = grp * QBLOCK
    band = 3 * QBLOCK

    kcat = jnp.concatenate([kp_ref[...], kc_ref[...], kn_ref[...]], axis=0)
    vcat = jnp.concatenate([vp_ref[...], vc_ref[...], vn_ref[...]], axis=0)
    ones = jnp.ones((kcat.shape[0], HEAD_DIM), BF16)
    for h in range(N_KV_HEADS):
        kh = kcat[:, h * HEAD_DIM:(h + 1) * HEAD_DIM]
        vh = vcat[:, h * HEAD_DIM:(h + 1) * HEAD_DIM]
        kd_ref[h] = jnp.concatenate([kh, kh], axis=1)
        vd_ref[2 * h] = jnp.concatenate([vh, ones], axis=1)
        vd_ref[2 * h + 1] = jnp.concatenate([ones, vh], axis=1)

    lane = lax.broadcasted_iota(I32, (QBLOCK, LANES), 1)
    low = lane < HEAD_DIM
    low2 = lax.broadcasted_iota(I32, (2 * QBLOCK, LANES), 1) < HEAD_DIM
    hrow = lax.broadcasted_iota(I32, (rows, 1), 0) // QBLOCK
    head_order = (0, 2, 1, 3)
    sinks = []
    for hk in range(N_KV_HEADS):
        sink = jnp.zeros((rows, 1), F32)
        for g in range(grp):
            sink = jnp.where(hrow == g, sink_ref[hk * grp + head_order[g]], sink)
        sinks.append(sink)

    def qblock(jb, carry):
        r0 = pl.multiple_of(jb * QBLOCK, QBLOCK)
        pos0 = i * tq + r0
        bias_prev = bias_ref[jnp.where(pos0 == 0, 1, 0)]
        bias_next = bias_ref[jnp.where(pos0 + QBLOCK == seq, 3, 2)]
        qb = q_ref[pl.ds(r0, QBLOCK), :]
        for hk in range(N_KV_HEADS):
            q2 = [qb[:, (2 * hk + cc) * LANES:(2 * hk + cc + 1) * LANES] for cc in range(2)]
            qs = jnp.concatenate(
                [jnp.where(low, q, jnp.zeros_like(q)) for q in q2]
                + [jnp.where(low, jnp.zeros_like(q), q) for q in q2], axis=0)
            sink = sinks[hk]
            s = lax.dot_general(qs, kd_ref[hk, pl.ds(r0, band), :], (((1,), (1,)), ((), ())),
                                preferred_element_type=F32)
            s = jnp.concatenate([s[:, :QBLOCK] + jnp.concatenate([bias_prev] * grp, axis=0),
                                 s[:, QBLOCK:2 * QBLOCK],
                                 s[:, 2 * QBLOCK:] + jnp.concatenate([bias_next] * grp, axis=0)],
                                axis=1)
            m = jnp.maximum(jnp.max(s, axis=1, keepdims=True), sink)
            p = jnp.exp((s - m).astype(BF16))
            sink_p = jnp.exp(sink - m)
            half_rows = 2 * QBLOCK
            o_lo = jnp.dot(p[:half_rows], vd_ref[2 * hk, pl.ds(r0, band), :],
                           preferred_element_type=F32)
            o_hi = jnp.dot(p[half_rows:], vd_ref[2 * hk + 1, pl.ds(r0, band), :],
                           preferred_element_type=F32)
            den_lo = jnp.where(low2, pltpu.roll(o_lo, HEAD_DIM, axis=1), o_lo) + sink_p[:half_rows]
            den_hi = jnp.where(low2, o_hi, pltpu.roll(o_hi, HEAD_DIM, axis=1)) + sink_p[half_rows:]
            y_lo = o_lo * (1.0 / den_lo)
            y_hi = o_hi * (1.0 / den_hi)
            for cc in range(2):
                c = 2 * hk + cc
                ya_ref[pl.ds(r0, QBLOCK), c * LANES:(c + 1) * LANES] = jnp.where(
                    low, y_lo[cc * QBLOCK:(cc + 1) * QBLOCK], y_hi[cc * QBLOCK:(cc + 1) * QBLOCK])
        return carry

    for jb in range(tq // QBLOCK):
        qblock(jb, 0)

    cu = cu_ref[...].astype(F32)
    halo = cup_ref.shape[0]
    prev_row = jnp.where(i > 0, cup_ref[halo - 1:halo, :].astype(F32), 0.0)
    next_row = jnp.where(i < nseq - 1, cun_ref[0:1, :].astype(F32), 0.0)
    ridx = lax.broadcasted_iota(I32, cu.shape, 0)
    up = jnp.where(ridx == 0, prev_row, pltpu.roll(cu, 1, axis=0))
    dn = jnp.where(ridx == tq - 1, next_row, pltpu.roll(cu, tq - 1, axis=0))
    conv = up * cw_ref[0:1, :] + cu * cw_ref[1:2, :] + dn * cw_ref[2:3, :]
    yc = _rms(b_ref[...].astype(F32) * conv, gc_ref[...]).astype(BF16)
    ya = _rms(ya_ref[...], ga_ref[...]).astype(BF16)
    o_ref[...] = (x_ref[...]
                  + jnp.dot(ya, wo_ref[:ATTN_WIDTH, :], preferred_element_type=F32)
                  + jnp.dot(yc, wo_ref[ATTN_WIDTH:, :], preferred_element_type=F32))
    for u in range(xs_ref.shape[0]):
        _route_tile(u, o_ref, gf_ref, wr_ref, br_ref, xs_ref, s_ref, meta_ref)


def _mix(x2, q, k, v, gb, cu, sink, conv_w, g_attn, g_conv, w_out_b, g_ffn, w_router, b_router,
         batch, seq):
    t, d = x2.shape
    tq = MIX_ROWS
    nseq = seq // tq
    per = tq // SORT_ROWS
    nt = t // SORT_ROWS
    tile = lambda b, i: (b * nseq + i, 0, 0)
    kb = tq // QBLOCK
    nkb = seq // QBLOCK
    halo = 16
    hb = tq // halo
    nhb = seq // halo

    row = lambda b, i: (b * nseq + i, 0)
    kprev = lambda b, i: (b * nkb + jnp.maximum(i * kb - 1, 0), 0)
    knext = lambda b, i: (b * nkb + jnp.minimum((i + 1) * kb, nkb - 1), 0)
    cprev = lambda b, i: (b * nhb + jnp.maximum(i * hb - 1, 0), 0)
    cnext = lambda b, i: (b * nhb + jnp.minimum((i + 1) * hb, nhb - 1), 0)
    fixed = lambda b, i: (0, 0)
    qi = lax.broadcasted_iota(I32, (QBLOCK, QBLOCK), 0)
    kj = lax.broadcasted_iota(I32, (QBLOCK, QBLOCK), 1)
    neg = jnp.full((QBLOCK, QBLOCK), NEG_INF, F32)
    band_bias = jnp.stack([jnp.where(kj - QBLOCK - qi >= -WINDOW, 0.0, NEG_INF).astype(F32), neg,
                           jnp.where(kj + QBLOCK - qi <= WINDOW, 0.0, NEG_INF).astype(F32), neg])
    return pl.pallas_call(
        functools.partial(_mix_kernel, seq=seq),
        grid=(batch, nseq),
        in_specs=[
            pl.BlockSpec(memory_space=pltpu.SMEM),
            pl.BlockSpec((tq, d), row),
            pl.BlockSpec((tq, ATTN_WIDTH), row),
            pl.BlockSpec((QBLOCK, KV_WIDTH), kprev),
            pl.BlockSpec((tq, KV_WIDTH), row),
            pl.BlockSpec((QBLOCK, KV_WIDTH), knext),
            pl.BlockSpec((QBLOCK, KV_WIDTH), kprev),
            pl.BlockSpec((tq, KV_WIDTH), row),
            pl.BlockSpec((QBLOCK, KV_WIDTH), knext),
            pl.BlockSpec((tq, CONV_WIDTH), row),
            pl.BlockSpec((tq, CONV_WIDTH), row),
            pl.BlockSpec((halo, CONV_WIDTH), cprev),
            pl.BlockSpec((halo, CONV_WIDTH), cnext),
            pl.BlockSpec(conv_w.shape, fixed),
            pl.BlockSpec((1, ATTN_WIDTH), fixed),
            pl.BlockSpec((1, CONV_WIDTH), fixed),
            pl.BlockSpec(w_out_b.shape, fixed),
            pl.BlockSpec(band_bias.shape, lambda b, i: (0, 0, 0)),
            pl.BlockSpec((1, d), fixed),
            pl.BlockSpec(w_router.shape, fixed),
            pl.BlockSpec(b_router.shape, fixed),
        ],
        out_specs=[
            pl.BlockSpec((tq, d), row),
            pl.BlockSpec((per, SORT_CAP, XS_W), tile),
            pl.BlockSpec((per, SUBLANES, SORT_ROWS), tile),
            pl.BlockSpec((per, SUBLANES, LANES), tile),
        ],
        out_shape=[
            jax.ShapeDtypeStruct((t, d), F32),
            jax.ShapeDtypeStruct((nt, SORT_CAP, XS_W), U32),
            jax.ShapeDtypeStruct((nt, SUBLANES, SORT_ROWS), I32),
            jax.ShapeDtypeStruct((nt, SUBLANES, LANES), I32),
        ],
        scratch_shapes=[pltpu.VMEM((tq, ATTN_WIDTH), F32),
                        pltpu.VMEM((N_KV_HEADS, tq + 2 * QBLOCK, LANES), BF16),
                        pltpu.VMEM((2 * N_KV_HEADS, tq + 2 * QBLOCK, LANES), BF16)],
        compiler_params=pltpu.CompilerParams(
            dimension_semantics=("parallel", "parallel"), vmem_limit_bytes=VMEM_LIMIT),
        name="mix",
    )(sink, x2, q, k, k, k, v, v, v, gb, cu, cu, cu, conv_w, g_attn, g_conv, w_out_b, band_bias,
      g_ffn, w_router, b_router)


ROUTER_ROWS = N_EXPERTS + SUBLANES


def _route_tile(u, x_ref, g_ref, w_ref, b_ref, xs_ref, s_ref, meta_ref):
    tl = SORT_ROWS
    h = _rms(x_ref[u * tl:(u + 1) * tl, :], g_ref[...])
    hb = h.astype(BF16)

    dn = (((1,), (1,)), ((), ()))
    h2 = (h - hb.astype(F32)).astype(BF16)
    both = lax.dot_general(w_ref[...], hb, dn, preferred_element_type=F32)
    corr = lax.dot_general(w_ref[:ROUTER_ROWS, :], h2, dn, preferred_element_type=F32)
    logits = both[:ROUTER_ROWS] + both[ROUTER_ROWS:] + corr + b_ref[...]
    el = logits[:N_EXPERTS]
    gl = logits[N_EXPERTS:]

    r8 = lax.broadcasted_iota(I32, gl.shape, 0)
    glm = jnp.where(r8 < N_GROUPS, gl, -jnp.inf)
    gmax = jnp.max(glm, axis=0, keepdims=True)
    grp = jnp.min(jnp.where(glm == gmax, r8, SUBLANES), axis=0, keepdims=True)
    p_grp = 1.0 / jnp.sum(jnp.exp(glm - gmax), axis=0, keepdims=True)

    r32 = lax.broadcasted_iota(I32, el.shape, 0)
    e0 = jnp.where((r32 // PER_GROUP) == grp, el, -jnp.inf)
    t1 = jnp.max(e0, axis=0, keepdims=True)
    i1 = jnp.min(jnp.where(e0 == t1, r32, N_EXPERTS), axis=0, keepdims=True)
    a1 = r32 == i1
    e1 = jnp.where(a1, -jnp.inf, e0)
    t2 = jnp.max(e1, axis=0, keepdims=True)
    i2 = jnp.min(jnp.where(e1 == t2, r32, N_EXPERTS), axis=0, keepdims=True)
    a2 = r32 == i2
    ed = jnp.exp(t2 - t1)
    inv = 1.0 / (1.0 + ed)
    g1 = p_grp * inv
    g2 = p_grp * (ed * inv)

    af = (a1 | a2).astype(F32)
    cnt = jnp.sum(af, axis=1, keepdims=True)
    pc = jnp.floor((cnt + (CHUNK - 1)) * (1.0 / CHUNK)) * CHUNK
    er = lax.broadcasted_iota(I32, (N_EXPERTS, LANES), 0)
    ec = lax.broadcasted_iota(I32, (N_EXPERTS, LANES), 1)
    pc_row = jnp.sum(jnp.where(er == ec, pc, 0.0), axis=0, keepdims=True)
    lo_row = jnp.sum(jnp.where(er < ec, pc, 0.0), axis=0, keepdims=True)
    lo_col = jnp.sum(jnp.where(ec < er, pc_row, 0.0), axis=1, keepdims=True)
    total = jnp.sum(pc, axis=0, keepdims=True)

    tr = lax.broadcasted_iota(I32, (tl, tl), 0)
    tc = lax.broadcasted_iota(I32, (tl, tl), 1)
    upper = (tr <= tc).astype(BF16)
    incl = jnp.dot(af.astype(BF16), upper, preferred_element_type=F32)
    slot = lo_col + incl - af
    s1 = jnp.sum(jnp.where(a1, slot, 0.0), axis=0, keepdims=True).astype(I32)
    s2 = jnp.sum(jnp.where(a2, slot, 0.0), axis=0, keepdims=True).astype(I32)

    cap = xs_ref.shape[1]
    srow = lax.broadcasted_iota(I32, (cap, tl), 0)
    p1 = srow == s1
    p2 = srow == s2
    perm = (p1 | p2).astype(BF16)
    half = hb.shape[1] // 2
    lo = jnp.dot(perm, hb[:, :half], preferred_element_type=F32)
    hi = jnp.dot(perm, hb[:, half:], preferred_element_type=F32)
    xs_ref[u, :, :PACK_W] = _pack2(lo, hi)
    gs = jnp.sum(jnp.where(p1, g1, 0.0) + jnp.where(p2, g2, 0.0), axis=1, keepdims=True)
    gs = jnp.broadcast_to(gs, (cap, LANES))
    gs_hi = gs.astype(BF16).astype(F32)
    xs_ref[u, :, PACK_W:] = _pack2(gs_hi, gs - gs_hi)

    sr = lax.broadcasted_iota(I32, (SUBLANES, tl), 0)
    s_ref[u] = jnp.where(sr == 0, s1, jnp.where(sr == 1, s2, 0))
    mr = lax.broadcasted_iota(I32, (SUBLANES, LANES), 0)
    meta = jnp.where(mr == 0, pc_row, jnp.where(mr == 1, lo_row, jnp.where(mr == 2, total, 0.0)))
    meta_ref[u] = (meta * (1.0 / CHUNK)).astype(I32)


def _router_params(w_group, b_group, w_expert, b_expert):
    pad = SUBLANES - N_GROUPS
    w = jnp.concatenate([jnp.swapaxes(w_expert, 1, 2),
                         jnp.pad(jnp.swapaxes(w_group, 1, 2), ((0, 0), (0, pad), (0, 0)))], axis=1)
    w1 = w.astype(BF16)
    w2 = (w - w1.astype(F32)).astype(BF16)
    b = jnp.concatenate([b_expert, jnp.pad(b_group, ((0, 0), (0, pad)))], axis=1)[:, :, None]
    return jnp.concatenate([w1, w2], axis=1), b


def _n_blocks(t):
    nt = t // SORT_ROWS
    max_chunks = (2 * t + (CHUNK - 1) * N_EXPERTS * nt) // CHUNK + (BLOCK_CHUNKS - 1) * N_EXPERTS
    return -(-max_chunks // BLOCK_CHUNKS)


PLAN_LANES = 1024


def _list_len(t):
    return -(-(_n_blocks(t) + 1) * BLOCK_CHUNKS // PLAN_LANES) * PLAN_LANES


def _plan_kernel(nch_ref, ncht_ref, lo8_ref, csrc_ref, cdst_ref, blk_ref, nu_ref):
    nt = nch_ref.shape[0]
    ch = csrc_ref.shape[1]
    nch = nch_ref[:, :N_EXPERTS].astype(F32)
    lo8 = lo8_ref[:, :N_EXPERTS].astype(F32)
    ncht = ncht_ref[...].astype(F32)

    tc = jnp.sum(ncht, axis=1, keepdims=True)
    ptc = jnp.floor((tc + (BLOCK_CHUNKS - 1)) * (1.0 / BLOCK_CHUNKS)) * BLOCK_CHUNKS
    er = lax.broadcasted_iota(I32, (N_EXPERTS, LANES), 0)
    ec = lax.broadcasted_iota(I32, (N_EXPERTS, LANES), 1)
    ptc_row = jnp.sum(jnp.where(er == ec, ptc, 0.0), axis=0, keepdims=True)
    eend = jnp.sum(jnp.where(ec <= er, ptc_row, 0.0), axis=1, keepdims=True)
    estart = eend - ptc
    total = jnp.sum(ptc, axis=0, keepdims=True)

    tr = lax.broadcasted_iota(I32, (nt, nt), 0)
    tcol = lax.broadcasted_iota(I32, (nt, nt), 1)
    lower = (tcol <= tr).astype(BF16)
    cum = jnp.dot(lower, nch.astype(BF16), preferred_element_type=F32)
    cum_hi = jnp.floor(cum * (1.0 / LANES))
    cum_lo = cum - cum_hi * LANES

    k = (pl.program_id(0) * ch + lax.broadcasted_iota(I32, (1, ch), 1)).astype(F32)
    r32 = lax.broadcasted_iota(I32, (N_EXPERTS, ch), 0).astype(F32)

    def expert_of(pos):
        return jnp.minimum(jnp.sum((eend <= pos).astype(F32), axis=0, keepdims=True),
                           N_EXPERTS - 1.0)

    e_k = expert_of(k)
    sel_e = r32 == e_k
    oh_e = sel_e.astype(BF16)
    start_k = jnp.sum(jnp.where(sel_e, estart, 0.0), axis=0, keepdims=True)
    tc_k = jnp.sum(jnp.where(sel_e, tc, 0.0), axis=0, keepdims=True)
    q = k - start_k

    def pick(tbl):
        return jnp.dot(tbl.astype(BF16), oh_e, preferred_element_type=F32)

    cum_e = pick(cum_hi) * LANES + pick(cum_lo)
    nch_e = pick(nch)
    lo8_e = pick(lo8)
    rnt = lax.broadcasted_iota(I32, (nt, ch), 0).astype(F32)
    i_k = jnp.minimum(jnp.sum((cum_e <= q).astype(F32), axis=0, keepdims=True), nt - 1.0)
    sel_i = rnt == i_k
    before = jnp.sum(jnp.where(sel_i, cum_e - nch_e, 0.0), axis=0, keepdims=True)
    lo8_k = jnp.sum(jnp.where(sel_i, lo8_e, 0.0), axis=0, keepdims=True)
    src = i_k * TILE_CHUNKS + lo8_k + (q - before)
    pad = (q >= tc_k) | (k >= total)
    csrc_ref[...] = jnp.where(pad, TILE_CHUNKS - 1.0, src).astype(I32)
    cdst_ref[...] = jnp.where(pad, nt * TILE_CHUNKS + k, src).astype(I32)

    n_used = total * (1.0 / BLOCK_CHUNKS)
    bpos = lax.broadcasted_iota(I32, blk_ref.shape, 1).astype(F32)
    blk_ref[...] = expert_of(jnp.minimum(bpos, n_used - 1.0) * BLOCK_CHUNKS).astype(I32)
    nu_ref[...] = jnp.broadcast_to(n_used, nu_ref.shape).astype(I32)


def _plan(meta, t):
    nt = t // SORT_ROWS
    nl = _list_len(t)
    nbp = -(-_n_blocks(t) // LANES) * LANES
    nch = meta[:, 0, :]
    lo8 = meta[:, 1, :]
    ncht = nch[:, :N_EXPERTS].T
    fixed = lambda c: (0, 0)
    csrc, cdst, blk_e, n_used = pl.pallas_call(
        _plan_kernel,
        grid=(nl // PLAN_LANES,),
        in_specs=[
            pl.BlockSpec(nch.shape, fixed),
            pl.BlockSpec(ncht.shape, fixed),
            pl.BlockSpec(lo8.shape, fixed),
        ],
        out_specs=[
            pl.BlockSpec((1, PLAN_LANES), lambda c: (0, c)),
            pl.BlockSpec((1, PLAN_LANES), lambda c: (0, c)),
            pl.BlockSpec((1, nbp), fixed),
            pl.BlockSpec((1, LANES), fixed),
        ],
        out_shape=[
            jax.ShapeDtypeStruct((1, nl), I32),
            jax.ShapeDtypeStruct((1, nl), I32),
            jax.ShapeDtypeStruct((1, nbp), I32),
            jax.ShapeDtypeStruct((1, LANES), I32),
        ],
        compiler_params=pltpu.CompilerParams(dimension_semantics=("arbitrary",)),
        name="plan",
    )(nch, ncht, lo8)
    return blk_e.reshape(-1), n_used.reshape(-1), csrc.reshape(-1), cdst.reshape(-1)


def _expert_kernel(be_ref, nu_ref, csrc_ref, cdst_ref,
                   xs_hbm, wg_ref, wu_ref, wd_ref, ys_hbm,
                   xbuf, ybuf, wgb, wub, wdb, sem_in, sem_out):
    b = pl.program_id(0)
    n_used = nu_ref[0]
    slot = b % 2

    def gather(blk, sl):
        return [pltpu.make_async_copy(xs_hbm.at[csrc_ref[blk * BLOCK_CHUNKS + j]],
                                      xbuf.at[sl, j], sem_in.at[sl])
                for j in range(BLOCK_CHUNKS)]

    def scatter(blk, sl):
        return [pltpu.make_async_copy(ybuf.at[sl, j],
                                      ys_hbm.at[cdst_ref[blk * BLOCK_CHUNKS + j]], sem_out.at[sl])
                for j in range(BLOCK_CHUNKS)]

    @pl.when(b == 0)
    def _():
        for c in gather(0, 0):
            c.start()

    new_expert = jnp.logical_or(b == 0, be_ref[b] != be_ref[jnp.maximum(b - 1, 0)])

    @pl.when(jnp.logical_and(b < n_used, new_expert))
    def _():
        wgb[...] = wg_ref[0, 0].astype(BF16)
        wub[...] = wu_ref[0, 0].astype(BF16)
        wdb[...] = wd_ref[0, 0].astype(BF16)

    @pl.when(b < n_used)
    def _():
        for c in gather(b, slot):
            c.wait()
        for c in gather(b + 1, 1 - slot):
            c.start()
        w = xbuf[slot].reshape(BLOCK_ROWS, XS_W)
        xw = w[:, :PACK_W]
        lo = _unpack2(xw, 0).astype(BF16)
        hi = _unpack2(xw, 1).astype(BF16)
        gw = w[:, PACK_W:]
        gate = (_unpack2(gw, 0) + _unpack2(gw, 1))[:, 0:1]
        half = PACK_W
        hg = (jnp.dot(lo, wgb[:half, :], preferred_element_type=F32)
              + jnp.dot(hi, wgb[half:, :], preferred_element_type=F32))
        hu = (jnp.dot(lo, wub[:half, :], preferred_element_type=F32)
              + jnp.dot(hi, wub[half:, :], preferred_element_type=F32))
        act = (hg * (1.0 / (1.0 + jnp.exp(-hg))) * hu).astype(BF16)
        y = jnp.dot(act, wdb[...], preferred_element_type=F32) * gate
        ybuf[slot] = _pack2(y[:, :half], y[:, half:]).reshape(BLOCK_CHUNKS, CHUNK, PACK_W)
        for c in scatter(b, slot):
            c.start()

    @pl.when(jnp.logical_and(b >= 1, b < n_used))
    def _():
        for c in scatter(b - 1, 1 - slot):
            c.wait()

    @pl.when(b == n_used - 1)
    def _():
        for c in scatter(b, slot):
            c.wait()
        for c in gather(b + 1, 1 - slot):
            c.wait()


def _experts(xs, blk_e, n_used, csrc, cdst, w_gate, w_up, w_down, layer, t):
    nt = t // SORT_ROWS
    nb = _n_blocks(t)
    d = w_gate.shape[2]
    xs3 = xs.reshape(nt * TILE_CHUNKS, CHUNK, XS_W)
    n_out = nt * TILE_CHUNKS + _list_len(t)
    wmap = lambda b, be, nu, cs, cd: (layer, be[b], 0, 0)
    grid_spec = pltpu.PrefetchScalarGridSpec(
        num_scalar_prefetch=4,
        grid=(nb,),
        in_specs=[
            pl.BlockSpec(memory_space=pl.ANY),
            pl.BlockSpec((1, 1, d, D_EXPERT), wmap),
            pl.BlockSpec((1, 1, d, D_EXPERT), wmap),
            pl.BlockSpec((1, 1, D_EXPERT, d), wmap),
        ],
        out_specs=pl.BlockSpec(memory_space=pl.ANY),
        scratch_shapes=[
            pltpu.VMEM((2, BLOCK_CHUNKS, CHUNK, XS_W), U32),
            pltpu.VMEM((2, BLOCK_CHUNKS, CHUNK, PACK_W), U32),
            pltpu.VMEM((d, D_EXPERT), BF16),
            pltpu.VMEM((d, D_EXPERT), BF16),
            pltpu.VMEM((D_EXPERT, d), BF16),
            pltpu.SemaphoreType.DMA((2,)),
            pltpu.SemaphoreType.DMA((2,)),
        ],
    )
    return pl.pallas_call(
        _expert_kernel,
        grid_spec=grid_spec,
        out_shape=jax.ShapeDtypeStruct((n_out, CHUNK, PACK_W), U32),
        compiler_params=pltpu.CompilerParams(
            dimension_semantics=("arbitrary",), vmem_limit_bytes=VMEM_LIMIT),
        name="experts",
    )(blk_e, n_used, csrc, cdst, xs3, w_gate, w_up, w_down)


def _combine_kernel(x_ref, ys_ref, s_ref, meta_ref, g_ref, o_ref):
    _combine_tiles(x_ref, ys_ref, s_ref, meta_ref, o_ref, g_ref)


def _combine_tiles(x_ref, ys_ref, s_ref, meta_ref, o_ref, g_ref):
    tl = SORT_ROWS
    cap = SORT_CAP
    tr = lax.broadcasted_iota(I32, (tl, tl), 0)
    tc = lax.broadcasted_iota(I32, (tl, tl), 1)
    eye = tr == tc
    col = lax.broadcasted_iota(I32, (tl, cap), 1)
    yrow = lax.broadcasted_iota(I32, (cap, PACK_W), 0)
    for u in range(s_ref.shape[0]):
        used = meta_ref[u, 2:3, 0:1] * CHUNK
        ys = ys_ref[u * cap:(u + 1) * cap, :]
        live = yrow < used
        ylo = jnp.where(live, _unpack2(ys, 0), 0.0).astype(BF16)
        yhi = jnp.where(live, _unpack2(ys, 1), 0.0).astype(BF16)
        s = s_ref[u]
        s1 = jnp.sum(jnp.where(eye, s[0:1, :], 0), axis=1, keepdims=True)
        s2 = jnp.sum(jnp.where(eye, s[1:2, :], 0), axis=1, keepdims=True)
        perm_t = ((col == s1) | (col == s2)).astype(BF16)
        half = PACK_W
        rows = slice(u * tl, (u + 1) * tl)
        out = jnp.concatenate(
            [x_ref[rows, :half] + jnp.dot(perm_t, ylo, preferred_element_type=F32),
             x_ref[rows, half:] + jnp.dot(perm_t, yhi, preferred_element_type=F32)], axis=1)
        if g_ref is not None:
            out = _rms(out, g_ref[...])
        o_ref[rows, :] = out


def _combine_norm(x2, ys, s12, meta, g_final):
    t, d = x2.shape
    tl = SORT_ROWS
    nt = t // tl
    ys2 = ys.reshape(ys.shape[0] * CHUNK, PACK_W)
    per = TILES_PER_STEP
    return pl.pallas_call(
        _combine_kernel,
        grid=(nt // per,),
        in_specs=[
            pl.BlockSpec((per * tl, d), lambda i: (i, 0)),
            pl.BlockSpec((per * SORT_CAP, PACK_W), lambda i: (i, 0)),
            pl.BlockSpec((per, SUBLANES, tl), lambda i: (i, 0, 0)),
            pl.BlockSpec((per, SUBLANES, LANES), lambda i: (i, 0, 0)),
            pl.BlockSpec((1, d), lambda i: (0, 0)),
        ],
        out_specs=pl.BlockSpec((per * tl, d), lambda i: (i, 0)),
        out_shape=jax.ShapeDtypeStruct((t, d), F32),
        compiler_params=pltpu.CompilerParams(
            dimension_semantics=("parallel",), vmem_limit_bytes=VMEM_LIMIT),
        name="combine",
    )(x2, ys2, s12, meta, g_final)


def _rope_tables(seq):
    pos = jnp.arange(seq, dtype=F32)
    inv_freq = ROPE_THETA ** (-jnp.arange(0, ROPE_DIM, 2, dtype=F32) / ROPE_DIM)
    ang = pos[:, None] * inv_freq[None, :]
    cos, sin = jnp.cos(ang), jnp.sin(ang)
    half = ROPE_DIM // 2
    ones = jnp.ones((seq, HEAD_DIM - ROPE_DIM), F32)
    zeros_h = jnp.zeros((seq, half), F32)
    zeros_r = jnp.zeros((seq, HEAD_DIM - ROPE_DIM), F32)
    cos_h = jnp.concatenate([cos, cos, ones], axis=1)
    sp_h = jnp.concatenate([zeros_h, sin, zeros_r], axis=1)
    sm_h = jnp.concatenate([-sin, zeros_h, zeros_r], axis=1)
    rep = LANES // HEAD_DIM
    return (jnp.tile(cos_h, (1, rep)), jnp.tile(sp_h, (1, rep)), jnp.tile(sm_h, (1, rep)))


def kernel(x, norm_mix, w_in, attn_sink, conv_w, norm_attn_out, norm_conv_out, w_out, norm_ffn,
           w_router_group, b_router_group, w_router_expert, b_router_expert,
           w_expert_gate, w_expert_up, w_expert_down, norm_final):
    batch, seq, d = x.shape
    depth = w_in.shape[0]
    t = batch * seq
    assert seq % MIX_ROWS == 0 and seq % FRONT_ROWS == 0 and t % SORT_ROWS == 0
    cos_t, sp_t, sm_t = _rope_tables(seq)
    x2 = x.reshape(t, d)
    w_in_b = w_in.astype(BF16)
    w_out_b = w_out.astype(BF16)
    w_router, b_router = _router_params(w_router_group, b_router_group,
                                        w_router_expert, b_router_expert)
    g_final = norm_final.reshape(1, d)

    ys = s12 = meta = None
    for l in range(depth):
        g_mix = norm_mix[l].reshape(1, d)
        if l == 0:
            q, k, v, gb, cu = _front(x2, g_mix, w_in_b[l], cos_t, sp_t, sm_t, seq)
        else:
            x2, q, k, v, gb, cu = _combine_front(x2, ys, s12, meta, g_mix, w_in_b[l],
                                                 cos_t, sp_t, sm_t, seq)
        x2, xs, s12, meta = _mix(x2, q, k, v, gb, cu, attn_sink[l], conv_w[l],
                                 norm_attn_out[l].reshape(1, -1), norm_conv_out[l].reshape(1, -1),
                                 w_out_b[l], norm_ffn[l].reshape(1, d), w_router[l], b_router[l],
                                 batch, seq)
        blk_e, n_used, csrc, cdst = _plan(meta, t)
        ys = _experts(xs, blk_e, n_used, csrc, cdst,
                      w_expert_gate, w_expert_up, w_expert_down, l, t)
    return _combine_norm(x2, ys, s12, meta, g_final).reshape(batch, seq, d)
```

```python
import functools

import jax
import jax.numpy as jnp
from jax import lax
from jax.experimental import pallas as pl
from jax.experimental.pallas import tpu as pltpu

F32 = jnp.float32
BF16 = jnp.bfloat16
U32 = jnp.uint32
I32 = jnp.int32

N_HEADS = 8
N_KV_HEADS = 2
HEAD_DIM = 64
ATTN_WIDTH = N_HEADS * HEAD_DIM
KV_WIDTH = N_KV_HEADS * HEAD_DIM
WINDOW = 128
QBLOCK = 128
ROPE_DIM = HEAD_DIM // 4
ROPE_THETA = 500000.0
CONV_WIDTH = 512
N_GROUPS = 4
PER_GROUP = 8
N_EXPERTS = N_GROUPS * PER_GROUP
D_EXPERT = 512
EPS = 1e-6
NEG_INF = -1e30

LANES = 128
SUBLANES = 8

FRONT_ROWS = 512
MIX_ROWS = 512
SORT_ROWS = 256
TILES_PER_STEP = 2
CHUNK = SUBLANES
BLOCK_ROWS = 512
BLOCK_CHUNKS = BLOCK_ROWS // CHUNK
SORT_CAP = ((2 * SORT_ROWS + N_EXPERTS * (CHUNK - 1)) // LANES + 1) * LANES
TILE_CHUNKS = SORT_CAP // CHUNK
PACK_W = 512
XS_W = PACK_W + LANES

VMEM_LIMIT = 56 * 1024 * 1024


def _rms(x, g):
    return x * lax.rsqrt(jnp.mean(x * x, axis=-1, keepdims=True) + EPS) * g


def _pack2(lo, hi):
    return pltpu.pack_elementwise([lo, hi], packed_dtype=BF16)


def _unpack2(w, index):
    return pltpu.unpack_elementwise(w, index=index, packed_dtype=BF16, unpacked_dtype=F32)


def _combine_front_kernel(x_ref, ys_ref, s_ref, meta_ref, g_ref, w_ref, cos_ref, sp_ref, sm_ref,
                          xo_ref, q_ref, k_ref, v_ref, b_ref, cu_ref):
    _combine_tiles(x_ref, ys_ref, s_ref, meta_ref, xo_ref, None)
    _front_kernel(xo_ref, g_ref, w_ref, cos_ref, sp_ref, sm_ref, q_ref, k_ref, v_ref, b_ref, cu_ref)


def _front_kernel(x_ref, g_ref, w_ref, cos_ref, sp_ref, sm_ref,
                  q_ref, k_ref, v_ref, b_ref, cu_ref):
    h = _rms(x_ref[...], g_ref[...]).astype(BF16)
    cos = cos_ref[...]
    sp = sp_ref[...]
    sm = sm_ref[...]

    def proj(lo, hi):
        return jnp.dot(h, w_ref[:, lo:hi], preferred_element_type=F32)

    def rope(t):
        half = ROPE_DIM // 2
        return (t * cos + pltpu.roll(t, half, axis=1) * sp
                + pltpu.roll(t, LANES - half, axis=1) * sm)

    wide = 2 * LANES
    for c in range(ATTN_WIDTH // wide):
        q = proj(c * wide, (c + 1) * wide)
        for j in range(2):
            q_ref[:, c * wide + j * LANES:c * wide + (j + 1) * LANES] = (
                rope(q[:, j * LANES:(j + 1) * LANES]) * (HEAD_DIM ** -0.5)).astype(BF16)
    o = ATTN_WIDTH
    kv = proj(o, o + 2 * KV_WIDTH)
    k_ref[...] = rope(kv[:, :KV_WIDTH]).astype(BF16)
    v_ref[...] = kv[:, KV_WIDTH:].astype(BF16)
    o += 2 * KV_WIDTH
    b_ref[...] = proj(o, o + CONV_WIDTH).astype(BF16)
    o += CONV_WIDTH
    gc = proj(o, o + CONV_WIDTH)
    o += CONV_WIDTH
    u = proj(o, o + CONV_WIDTH)
    cu_ref[...] = (gc * u).astype(BF16)


def _combine_front(x2, ys, s12, meta, g, w_in_b, cos_t, sp_t, sm_t, seq):
    t, d = x2.shape
    tm = FRONT_ROWS
    per = tm // SORT_ROWS
    nseq = seq // tm
    row = lambda i: (i, 0)
    tile = lambda i: (i, 0, 0)
    pos = lambda i: (i % nseq, 0)
    fixed = lambda i: (0, 0)
    ys2 = ys.reshape(ys.shape[0] * CHUNK, PACK_W)
    return pl.pallas_call(
        _combine_front_kernel,
        grid=(t // tm,),
        in_specs=[
            pl.BlockSpec((tm, d), row),
            pl.BlockSpec((per * SORT_CAP, PACK_W), row),
            pl.BlockSpec((per, SUBLANES, SORT_ROWS), tile),
            pl.BlockSpec((per, SUBLANES, LANES), tile),
            pl.BlockSpec((1, d), fixed),
            pl.BlockSpec(w_in_b.shape, fixed),
            pl.BlockSpec((tm, LANES), pos),
            pl.BlockSpec((tm, LANES), pos),
            pl.BlockSpec((tm, LANES), pos),
        ],
        out_specs=[
            pl.BlockSpec((tm, d), row),
            pl.BlockSpec((tm, ATTN_WIDTH), row),
            pl.BlockSpec((tm, KV_WIDTH), row),
            pl.BlockSpec((tm, KV_WIDTH), row),
            pl.BlockSpec((tm, CONV_WIDTH), row),
            pl.BlockSpec((tm, CONV_WIDTH), row),
        ],
        out_shape=[
            jax.ShapeDtypeStruct((t, d), F32),
            jax.ShapeDtypeStruct((t, ATTN_WIDTH), BF16),
            jax.ShapeDtypeStruct((t, KV_WIDTH), BF16),
            jax.ShapeDtypeStruct((t, KV_WIDTH), BF16),
            jax.ShapeDtypeStruct((t, CONV_WIDTH), BF16),
            jax.ShapeDtypeStruct((t, CONV_WIDTH), BF16),
        ],
        compiler_params=pltpu.CompilerParams(
            dimension_semantics=("parallel",), vmem_limit_bytes=VMEM_LIMIT),
        name="combine_front",
    )(x2, ys2, s12, meta, g, w_in_b, cos_t, sp_t, sm_t)


def _front(x2, g, w_in_b, cos_t, sp_t, sm_t, seq):
    t, d = x2.shape
    tm = FRONT_ROWS
    nseq = seq // tm
    row = lambda i: (i, 0)
    pos = lambda i: (i % nseq, 0)
    fixed = lambda i: (0, 0)
    return pl.pallas_call(
        _front_kernel,
        grid=(t // tm,),
        in_specs=[
            pl.BlockSpec((tm, d), row),
            pl.BlockSpec((1, d), fixed),
            pl.BlockSpec(w_in_b.shape, fixed),
            pl.BlockSpec((tm, LANES), pos),
            pl.BlockSpec((tm, LANES), pos),
            pl.BlockSpec((tm, LANES), pos),
        ],
        out_specs=[
            pl.BlockSpec((tm, ATTN_WIDTH), row),
            pl.BlockSpec((tm, KV_WIDTH), row),
            pl.BlockSpec((tm, KV_WIDTH), row),
            pl.BlockSpec((tm, CONV_WIDTH), row),
            pl.BlockSpec((tm, CONV_WIDTH), row),
        ],
        out_shape=[
            jax.ShapeDtypeStruct((t, ATTN_WIDTH), BF16),
            jax.ShapeDtypeStruct((t, KV_WIDTH), BF16),
            jax.ShapeDtypeStruct((t, KV_WIDTH), BF16),
            jax.ShapeDtypeStruct((t, CONV_WIDTH), BF16),
            jax.ShapeDtypeStruct((t, CONV_WIDTH), BF16),
        ],
        compiler_params=pltpu.CompilerParams(
            dimension_semantics=("parallel",), vmem_limit_bytes=VMEM_LIMIT),
        name="front",
    )(x2, g, w_in_b, cos_t, sp_t, sm_t)


def _mix_kernel(sink_ref, x_ref, q_ref, kp_ref, kc_ref, kn_ref, vp_ref, vc_ref, vn_ref,
                b_ref, cu_ref, cup_ref, cun_ref, cw_ref, ga_ref, gc_ref, wo_ref, bias_ref,
                gf_ref, wr_ref, br_ref,
                o_ref, xs_ref, s_ref, meta_ref, ya_ref, kd_ref, vd_ref, *, seq):
    tq = x_ref.shape[0]
    i = pl.program_id(1)
    nseq = pl.num_programs(1)
    grp = N_HEADS // N_KV_HEADS
    rows = grp * QBLOCK
    band = 3 * QBLOCK

    kcat = jnp.concatenate([kp_ref[...], kc_ref[...], kn_ref[...]], axis=0)
    vcat = jnp.concatenate([vp_ref[...], vc_ref[...], vn_ref[...]], axis=0)
    ones = jnp.ones((kcat.shape[0], HEAD_DIM), BF16)
    for h in range(N_KV_HEADS):
        kh = kcat[:, h * HEAD_DIM:(h + 1) * HEAD_DIM]
        vh = vcat[:, h * HEAD_DIM:(h + 1) * HEAD_DIM]
        kd_ref[h] = jnp.concatenate([kh, kh], axis=1)
        vd_ref[2 * h] = jnp.concatenate([vh, ones], axis=1)
        vd_ref[2 * h + 1] = jnp.concatenate([ones, vh], axis=1)

    lane = lax.broadcasted_iota(I32, (QBLOCK, LANES), 1)
    low = lane < HEAD_DIM
    low2 = lax.broadcasted_iota(I32, (2 * QBLOCK, LANES), 1) < HEAD_DIM
    hrow = lax.broadcasted_iota(I32, (rows, 1), 0) // QBLOCK
    head_order = (0, 2, 1, 3)
    sinks = []
    for hk in range(N_KV_HEADS):
        sink = jnp.zeros((rows, 1), F32)
        for g in range(grp):
            sink = jnp.where(hrow == g, sink_ref[hk * grp + head_order[g]], sink)
        sinks.append(sink)

    def qblock(jb, carry):
        r0 = pl.multiple_of(jb * QBLOCK, QBLOCK)
        pos0 = i * tq + r0
        bias_prev = bias_ref[jnp.where(pos0 == 0, 1, 0)]
        bias_next = bias_ref[jnp.where(pos0 + QBLOCK == seq, 3, 2)]
        qb = q_ref[pl.ds(r0, QBLOCK), :]
        for hk in range(N_KV_HEADS):
            q2 = [qb[:, (2 * hk + cc) * LANES:(2 * hk + cc + 1) * LANES] for cc in range(2)]
            qs = jnp.concatenate(
                [jnp.where(low, q, jnp.zeros_like(q)) for q in q2]
                + [jnp.where(low, jnp.zeros_like(q), q) for q in q2], axis=0)
            sink = sinks[hk]
            s = lax.dot_general(qs, kd_ref[hk, pl.ds(r0, band), :], (((1,), (1,)), ((), ())),
                                preferred_element_type=F32)
            s = jnp.concatenate([s[:, :QBLOCK] + jnp.concatenate([bias_prev] * grp, axis=0),
                                 s[:, QBLOCK:2 * QBLOCK],
                                 s[:, 2 * QBLOCK:] + jnp.concatenate([bias_next] * grp, axis=0)],
                                axis=1)
            m = jnp.maximum(jnp.max(s, axis=1, keepdims=True), sink)
            p = jnp.exp((s - m).astype(BF16))
            sink_p = jnp.exp(sink - m)
            half_rows = 2 * QBLOCK
            o_lo = jnp.dot(p[:half_rows], vd_ref[2 * hk, pl.ds(r0, band), :],
                           preferred_element_type=F32)
            o_hi = jnp.dot(p[half_rows:], vd_ref[2 * hk + 1, pl.ds(r0, band), :],
                           preferred_element_type=F32)
            den_lo = jnp.where(low2, pltpu.roll(o_lo, HEAD_DIM, axis=1), o_lo) + sink_p[:half_rows]
            den_hi = jnp.where(low2, o_hi, pltpu.roll(o_hi, HEAD_DIM, axis=1)) + sink_p[half_rows:]
            y_lo = o_lo * (1.0 / den_lo)
            y_hi = o_hi * (1.0 / den_hi)
            for cc in range(2):
                c = 2 * hk + cc
                ya_ref[pl.ds(r0, QBLOCK), c * LANES:(c + 1) * LANES] = jnp.where(
                    low, y_lo[cc * QBLOCK:(cc + 1) * QBLOCK], y_hi[cc * QBLOCK:(cc + 1) * QBLOCK])
        return carry

    for jb in range(tq // QBLOCK):
        qblock(jb, 0)

    cu = cu_ref[...].astype(F32)
    halo = cup_ref.shape[0]
    prev_row = jnp.where(i > 0, cup_ref[halo - 1:halo, :].astype(F32), 0.0)
    next_row = jnp.where(i < nseq - 1, cun_ref[0:1, :].astype(F32), 0.0)
    ridx = lax.broadcasted_iota(I32, cu.shape, 0)
    up = jnp.where(ridx == 0, prev_row, pltpu.roll(cu, 1, axis=0))
    dn = jnp.where(ridx == tq - 1, next_row, pltpu.roll(cu, tq - 1, axis=0))
    conv = up * cw_ref[0:1, :] + cu * cw_ref[1:2, :] + dn * cw_ref[2:3, :]
    yc = _rms(b_ref[...].astype(F32) * conv, gc_ref[...]).astype(BF16)
    ya = _rms(ya_ref[...], ga_ref[...]).astype(BF16)
    o_ref[...] = (x_ref[...]
                  + jnp.dot(ya, wo_ref[:ATTN_WIDTH, :], preferred_element_type=F32)
                  + jnp.dot(yc, wo_ref[ATTN_WIDTH:, :], preferred_element_type=F32))
    for u in range(xs_ref.shape[0]):
        _route_tile(u, o_ref, gf_ref, wr_ref, br_ref, xs_ref, s_ref, meta_ref)


def _mix(x2, q, k, v, gb, cu, sink, conv_w, g_attn, g_conv, w_out_b, g_ffn, w_router, b_router,
         batch, seq):
    t, d = x2.shape
    tq = MIX_ROWS
    nseq = seq // tq
    per = tq // SORT_ROWS
    nt = t // SORT_ROWS
    tile = lambda b, i: (b * nseq + i, 0, 0)
    kb = tq // QBLOCK
    nkb = seq // QBLOCK
    halo = 16
    hb = tq // halo
    nhb = seq // halo

    row = lambda b, i: (b * nseq + i, 0)
    kprev = lambda b, i: (b * nkb + jnp.maximum(i * kb - 1, 0), 0)
    knext = lambda b, i: (b * nkb + jnp.minimum((i + 1) * kb, nkb - 1), 0)
    cprev = lambda b, i: (b * nhb + jnp.maximum(i * hb - 1, 0), 0)
    cnext = lambda b, i: (b * nhb + jnp.minimum((i + 1) * hb, nhb - 1), 0)
    fixed = lambda b, i: (0, 0)
    qi = lax.broadcasted_iota(I32, (QBLOCK, QBLOCK), 0)
    kj = lax.broadcasted_iota(I32, (QBLOCK, QBLOCK), 1)
    neg = jnp.full((QBLOCK, QBLOCK), NEG_INF, F32)
    band_bias = jnp.stack([jnp.where(kj - QBLOCK - qi >= -WINDOW, 0.0, NEG_INF).astype(F32), neg,
                           jnp.where(kj + QBLOCK - qi <= WINDOW, 0.0, NEG_INF).astype(F32), neg])
    return pl.pallas_call(
        functools.partial(_mix_kernel, seq=seq),
        grid=(batch, nseq),
        in_specs=[
            pl.BlockSpec(memory_space=pltpu.SMEM),
            pl.BlockSpec((tq, d), row),
            pl.BlockSpec((tq, ATTN_WIDTH), row),
            pl.BlockSpec((QBLOCK, KV_WIDTH), kprev),
            pl.BlockSpec((tq, KV_WIDTH), row),
            pl.BlockSpec((QBLOCK, KV_WIDTH), knext),
            pl.BlockSpec((QBLOCK, KV_WIDTH), kprev),
            pl.BlockSpec((tq, KV_WIDTH), row),
            pl.BlockSpec((QBLOCK, KV_WIDTH), knext),
            pl.BlockSpec((tq, CONV_WIDTH), row),
            pl.BlockSpec((tq, CONV_WIDTH), row),
            pl.BlockSpec((halo, CONV_WIDTH), cprev),
            pl.BlockSpec((halo, CONV_WIDTH), cnext),
            pl.BlockSpec(conv_w.shape, fixed),
            pl.BlockSpec((1, ATTN_WIDTH), fixed),
            pl.BlockSpec((1, CONV_WIDTH), fixed),
            pl.BlockSpec(w_out_b.shape, fixed),
            pl.BlockSpec(band_bias.shape, lambda b, i: (0, 0, 0)),
            pl.BlockSpec((1, d), fixed),
            pl.BlockSpec(w_router.shape, fixed),
            pl.BlockSpec(b_router.shape, fixed),
        ],
        out_specs=[
            pl.BlockSpec((tq, d), row),
            pl.BlockSpec((per, SORT_CAP, XS_W), tile),
            pl.BlockSpec((per, SUBLANES, SORT_ROWS), tile),
            pl.BlockSpec((per, SUBLANES, LANES), tile),
        ],
        out_shape=[
            jax.ShapeDtypeStruct((t, d), F32),
            jax.ShapeDtypeStruct((nt, SORT_CAP, XS_W), U32),
            jax.ShapeDtypeStruct((nt, SUBLANES, SORT_ROWS), I32),
            jax.ShapeDtypeStruct((nt, SUBLANES, LANES), I32),
        ],
        scratch_shapes=[pltpu.VMEM((tq, ATTN_WIDTH), F32),
                        pltpu.VMEM((N_KV_HEADS, tq + 2 * QBLOCK, LANES), BF16),
                        pltpu.VMEM((2 * N_KV_HEADS, tq + 2 * QBLOCK, LANES), BF16)],
        compiler_params=pltpu.CompilerParams(
            dimension_semantics=("parallel", "parallel"), vmem_limit_bytes=VMEM_LIMIT),
        name="mix",
    )(sink, x2, q, k, k, k, v, v, v, gb, cu, cu, cu, conv_w, g_attn, g_conv, w_out_b, band_bias,
      g_ffn, w_router, b_router)


ROUTER_ROWS = N_EXPERTS + SUBLANES


def _route_tile(u, x_ref, g_ref, w_ref, b_ref, xs_ref, s_ref, meta_ref):
    tl = SORT_ROWS
    h = _rms(x_ref[u * tl:(u + 1) * tl, :], g_ref[...])
    hb = h.astype(BF16)

    dn = (((1,), (1,)), ((), ()))
    h2 = (h - hb.astype(F32)).astype(BF16)
    both = lax.dot_general(w_ref[...], hb, dn, preferred_element_type=F32)
    corr = lax.dot_general(w_ref[:ROUTER_ROWS, :], h2, dn, preferred_element_type=F32)
    logits = both[:ROUTER_ROWS] + both[ROUTER_ROWS:] + corr + b_ref[...]
    el = logits[:N_EXPERTS]
    gl = logits[N_EXPERTS:]

    r8 = lax.broadcasted_iota(I32, gl.shape, 0)
    glm = jnp.where(r8 < N_GROUPS, gl, -jnp.inf)
    gmax = jnp.max(glm, axis=0, keepdims=True)
    grp = jnp.min(jnp.where(glm == gmax, r8, SUBLANES), axis=0, keepdims=True)
    p_grp = 1.0 / jnp.sum(jnp.exp(glm - gmax), axis=0, keepdims=True)

    r32 = lax.broadcasted_iota(I32, el.shape, 0)
    e0 = jnp.where((r32 // PER_GROUP) == grp, el, -jnp.inf)
    t1 = jnp.max(e0, axis=0, keepdims=True)
    i1 = jnp.min(jnp.where(e0 == t1, r32, N_EXPERTS), axis=0, keepdims=True)
    a1 = r32 == i1
    e1 = jnp.where(a1, -jnp.inf, e0)
    t2 = jnp.max(e1, axis=0, keepdims=True)
    i2 = jnp.min(jnp.where(e1 == t2, r32, N_EXPERTS), axis=0, keepdims=True)
    a2 = r32 == i2
    ed = jnp.exp(t2 - t1)
    inv = 1.0 / (1.0 + ed)
    g1 = p_grp * inv
    g2 = p_grp * (ed * inv)

    af = (a1 | a2).astype(F32)
    cnt = jnp.sum(af, axis=1, keepdims=True)
    pc = jnp.floor((cnt + (CHUNK - 1)) * (1.0 / CHUNK)) * CHUNK
    er = lax.broadcasted_iota(I32, (N_EXPERTS, LANES), 0)
    ec = lax.broadcasted_iota(I32, (N_EXPERTS, LANES), 1)
    pc_row = jnp.sum(jnp.where(er == ec, pc, 0.0), axis=0, keepdims=True)
    lo_row = jnp.sum(jnp.where(er < ec, pc, 0.0), axis=0, keepdims=True)
    lo_col = jnp.sum(jnp.where(ec < er, pc_row, 0.0), axis=1, keepdims=True)
    total = jnp.sum(pc, axis=0, keepdims=True)

    tr = lax.broadcasted_iota(I32, (tl, tl), 0)
    tc = lax.broadcasted_iota(I32, (tl, tl), 1)
    upper = (tr <= tc).astype(BF16)
    incl = jnp.dot(af.astype(BF16), upper, preferred_element_type=F32)
    slot = lo_col + incl - af
    s1 = jnp.sum(jnp.where(a1, slot, 0.0), axis=0, keepdims=True).astype(I32)
    s2 = jnp.sum(jnp.where(a2, slot, 0.0), axis=0, keepdims=True).astype(I32)

    cap = xs_ref.shape[1]
    srow = lax.broadcasted_iota(I32, (cap, tl), 0)
    p1 = srow == s1
    p2 = srow == s2
    perm = (p1 | p2).astype(BF16)
    half = hb.shape[1] // 2
    lo = jnp.dot(perm, hb[:, :half], preferred_element_type=F32)
    hi = jnp.dot(perm, hb[:, half:], preferred_element_type=F32)
    xs_ref[u, :, :PACK_W] = _pack2(lo, hi)
    gs = jnp.sum(jnp.where(p1, g1, 0.0) + jnp.where(p2, g2, 0.0), axis=1, keepdims=True)
    gs = jnp.broadcast_to(gs, (cap, LANES))
    gs_hi = gs.astype(BF16).astype(F32)
    xs_ref[u, :, PACK_W:] = _pack2(gs_hi, gs - gs_hi)

    sr = lax.broadcasted_iota(I32, (SUBLANES, tl), 0)
    s_ref[u] = jnp.where(sr == 0, s1, jnp.where(sr == 1, s2, 0))
    mr = lax.broadcasted_iota(I32, (SUBLANES, LANES), 0)
    meta = jnp.where(mr == 0, pc_row, jnp.where(mr == 1, lo_row, jnp.where(mr == 2, total, 0.0)))
    meta_ref[u] = (meta * (1.0 / CHUNK)).astype(I32)


def _router_params(w_group, b_group, w_expert, b_expert):
    pad = SUBLANES - N_GROUPS
    w = jnp.concatenate([jnp.swapaxes(w_expert, 1, 2),
                         jnp.pad(jnp.swapaxes(w_group, 1, 2), ((0, 0), (0, pad), (0, 0)))], axis=1)
    w1 = w.astype(BF16)
    w2 = (w - w1.astype(F32)).astype(BF16)
    b = jnp.concatenate([b_expert, jnp.pad(b_group, ((0, 0), (0, pad)))], axis=1)[:, :, None]
    return jnp.concatenate([w1, w2], axis=1), b


def _n_blocks(t):
    nt = t // SORT_ROWS
    max_chunks = (2 * t + (CHUNK - 1) * N_EXPERTS * nt) // CHUNK + (BLOCK_CHUNKS - 1) * N_EXPERTS
    return -(-max_chunks // BLOCK_CHUNKS)


PLAN_LANES = 1024


def _list_len(t):
    return -(-(_n_blocks(t) + 1) * BLOCK_CHUNKS // PLAN_LANES) * PLAN_LANES


def _plan_kernel(nch_ref, ncht_ref, lo8_ref, csrc_ref, cdst_ref, bend_ref):
    nt = nch_ref.shape[0]
    ch = csrc_ref.shape[1]
    nch = nch_ref[:, :N_EXPERTS].astype(F32)
    lo8 = lo8_ref[:, :N_EXPERTS].astype(F32)
    ncht = ncht_ref[...].astype(F32)

    tc = jnp.sum(ncht, axis=1, keepdims=True)
    ptc = jnp.floor((tc + (BLOCK_CHUNKS - 1)) * (1.0 / BLOCK_CHUNKS)) * BLOCK_CHUNKS
    er = lax.broadcasted_iota(I32, (N_EXPERTS, LANES), 0)
    ec = lax.broadcasted_iota(I32, (N_EXPERTS, LANES), 1)
    ptc_row = jnp.sum(jnp.where(er == ec, ptc, 0.0), axis=0, keepdims=True)
    eend = jnp.sum(jnp.where(ec <= er, ptc_row, 0.0), axis=1, keepdims=True)
    estart = eend - ptc
    total = jnp.sum(ptc, axis=0, keepdims=True)

    tr = lax.broadcasted_iota(I32, (nt, nt), 0)
    tcol = lax.broadcasted_iota(I32, (nt, nt), 1)
    lower = (tcol <= tr).astype(BF16)
    cum = jnp.dot(lower, nch.astype(BF16), preferred_element_type=F32)
    cum_hi = jnp.floor(cum * (1.0 / LANES))
    cum_lo = cum - cum_hi * LANES

    k = (pl.program_id(0) * ch + lax.broadcasted_iota(I32, (1, ch), 1)).astype(F32)
    r32 = lax.broadcasted_iota(I32, (N_EXPERTS, ch), 0).astype(F32)

    def expert_of(pos):
        return jnp.minimum(jnp.sum((eend <= pos).astype(F32), axis=0, keepdims=True),
                           N_EXPERTS - 1.0)

    e_k = expert_of(k)
    sel_e = r32 == e_k
    oh_e = sel_e.astype(BF16)
    start_k = jnp.sum(jnp.where(sel_e, estart, 0.0), axis=0, keepdims=True)
    tc_k = jnp.sum(jnp.where(sel_e, tc, 0.0), axis=0, keepdims=True)
    q = k - start_k

    def pick(tbl):
        return jnp.dot(tbl.astype(BF16), oh_e, preferred_element_type=F32)

    cum_e = pick(cum_hi) * LANES + pick(cum_lo)
    nch_e = pick(nch)
    lo8_e = pick(lo8)
    rnt = lax.broadcasted_iota(I32, (nt, ch), 0).astype(F32)
    i_k = jnp.minimum(jnp.sum((cum_e <= q).astype(F32), axis=0, keepdims=True), nt - 1.0)
    sel_i = rnt == i_k
    before = jnp.sum(jnp.where(sel_i, cum_e - nch_e, 0.0), axis=0, keepdims=True)
    lo8_k = jnp.sum(jnp.where(sel_i, lo8_e, 0.0), axis=0, keepdims=True)
    src = i_k * TILE_CHUNKS + lo8_k + (q - before)
    pad = (q >= tc_k) | (k >= total)
    csrc_ref[...] = jnp.where(pad, TILE_CHUNKS - 1.0, src).astype(I32)
    cdst_ref[...] = jnp.where(pad, nt * TILE_CHUNKS + k, src).astype(I32)

    eend_row = jnp.sum(jnp.where(er <= ec, ptc, 0.0), axis=0, keepdims=True)
    bend_ref[...] = (eend_row * (1.0 / BLOCK_CHUNKS)).astype(I32)


def _plan(meta, t):
    nt = t // SORT_ROWS
    nl = _list_len(t)
    nch = meta[:, 0, :]
    lo8 = meta[:, 1, :]
    ncht = nch[:, :N_EXPERTS].T
    fixed = lambda c: (0, 0)
    csrc, cdst, bend = pl.pallas_call(
        _plan_kernel,
        grid=(nl // PLAN_LANES,),
        in_specs=[
            pl.BlockSpec(nch.shape, fixed),
            pl.BlockSpec(ncht.shape, fixed),
            pl.BlockSpec(lo8.shape, fixed),
        ],
        out_specs=[
            pl.BlockSpec((1, PLAN_LANES), lambda c: (0, c)),
            pl.BlockSpec((1, PLAN_LANES), lambda c: (0, c)),
            pl.BlockSpec((1, LANES), fixed),
        ],
        out_shape=[
            jax.ShapeDtypeStruct((1, nl), I32),
            jax.ShapeDtypeStruct((1, nl), I32),
            jax.ShapeDtypeStruct((1, LANES), I32),
        ],
        compiler_params=pltpu.CompilerParams(dimension_semantics=("arbitrary",)),
        name="plan",
    )(nch, ncht, lo8)
    return bend.reshape(-1), csrc.reshape(-1), cdst.reshape(-1)


def _expert_kernel(bend_ref, csrc_ref, cdst_ref,
                   xs_hbm, wg_ref, wu_ref, wd_ref, ys_hbm,
                   xbuf, ybuf, wgb, wub, wdb, sem_in, sem_out):
    e = pl.program_id(0)
    n_used = bend_ref[N_EXPERTS - 1]
    b_lo = jnp.where(e == 0, 0, bend_ref[jnp.maximum(e - 1, 0)])
    b_hi = bend_ref[e]

    def gather(blk, sl):
        return [pltpu.make_async_copy(xs_hbm.at[csrc_ref[blk * BLOCK_CHUNKS + j]],
                                      xbuf.at[sl, j], sem_in.at[sl])
                for j in range(BLOCK_CHUNKS)]

    def scatter(blk, sl):
        return [pltpu.make_async_copy(ybuf.at[sl, j],
                                      ys_hbm.at[cdst_ref[blk * BLOCK_CHUNKS + j]], sem_out.at[sl])
                for j in range(BLOCK_CHUNKS)]

    @pl.when(e == 0)
    def _():
        for c in gather(0, 0):
            c.start()

    @pl.when(b_hi > b_lo)
    def _():
        wgb[...] = wg_ref[0, 0].astype(BF16)
        wub[...] = wu_ref[0, 0].astype(BF16)
        wdb[...] = wd_ref[0, 0].astype(BF16)

    def block(b, carry):
        slot = lax.rem(b, 2)
        for c in gather(b, slot):
            c.wait()
        for c in gather(b + 1, 1 - slot):
            c.start()
        w = xbuf[slot].reshape(BLOCK_ROWS, XS_W)
        xw = w[:, :PACK_W]
        lo = _unpack2(xw, 0).astype(BF16)
        hi = _unpack2(xw, 1).astype(BF16)
        gw = w[:, PACK_W:]
        gate = (_unpack2(gw, 0) + _unpack2(gw, 1))[:, 0:1]
        half = PACK_W
        hg = (jnp.dot(lo, wgb[:half, :], preferred_element_type=F32)
              + jnp.dot(hi, wgb[half:, :], preferred_element_type=F32))
        hu = (jnp.dot(lo, wub[:half, :], preferred_element_type=F32)
              + jnp.dot(hi, wub[half:, :], preferred_element_type=F32))
        act = (hg * (1.0 / (1.0 + jnp.exp(-hg))) * hu).astype(BF16)
        y = jnp.dot(act, wdb[...], preferred_element_type=F32) * gate
        ybuf[slot] = _pack2(y[:, :half], y[:, half:]).reshape(BLOCK_CHUNKS, CHUNK, PACK_W)
        for c in scatter(b, slot):
            c.start()

        @pl.when(b >= 1)
        def _():
            for c in scatter(b - 1, 1 - slot):
                c.wait()

        return carry

    lax.fori_loop(b_lo, b_hi, block, 0)

    @pl.when(e == N_EXPERTS - 1)
    def _():
        last = n_used - 1
        for c in scatter(last, lax.rem(last, 2)):
            c.wait()
        for c in gather(n_used, lax.rem(n_used, 2)):
            c.wait()


def _experts(xs, bend, csrc, cdst, w_gate, w_up, w_down, layer, t):
    nt = t // SORT_ROWS
    d = w_gate.shape[2]
    xs3 = xs.reshape(nt * TILE_CHUNKS, CHUNK, XS_W)
    n_out = nt * TILE_CHUNKS + _list_len(t)
    wmap = lambda e, be, cs, cd: (layer, e, 0, 0)
    grid_spec = pltpu.PrefetchScalarGridSpec(
        num_scalar_prefetch=3,
        grid=(N_EXPERTS,),
        in_specs=[
            pl.BlockSpec(memory_space=pl.ANY),
            pl.BlockSpec((1, 1, d, D_EXPERT), wmap),
            pl.BlockSpec((1, 1, d, D_EXPERT), wmap),
            pl.BlockSpec((1, 1, D_EXPERT, d), wmap),
        ],
        out_specs=pl.BlockSpec(memory_space=pl.ANY),
        scratch_shapes=[
            pltpu.VMEM((2, BLOCK_CHUNKS, CHUNK, XS_W), U32),
            pltpu.VMEM((2, BLOCK_CHUNKS, CHUNK, PACK_W), U32),
            pltpu.VMEM((d, D_EXPERT), BF16),
            pltpu.VMEM((d, D_EXPERT), BF16),
            pltpu.VMEM((D_EXPERT, d), BF16),
            pltpu.SemaphoreType.DMA((2,)),
            pltpu.SemaphoreType.DMA((2,)),
        ],
    )
    return pl.pallas_call(
        _expert_kernel,
        grid_spec=grid_spec,
        out_shape=jax.ShapeDtypeStruct((n_out, CHUNK, PACK_W), U32),
        compiler_params=pltpu.CompilerParams(
            dimension_semantics=("arbitrary",), vmem_limit_bytes=VMEM_LIMIT),
        name="experts",
    )(bend, csrc, cdst, xs3, w_gate, w_up, w_down)


def _combine_kernel(x_ref, ys_ref, s_ref, meta_ref, g_ref, o_ref):
    _combine_tiles(x_ref, ys_ref, s_ref, meta_ref, o_ref, g_ref)


def _combine_tiles(x_ref, ys_ref, s_ref, meta_ref, o_ref, g_ref):
    tl = SORT_ROWS
    cap = SORT_CAP
    tr = lax.broadcasted_iota(I32, (tl, tl), 0)
    tc = lax.broadcasted_iota(I32, (tl, tl), 1)
    eye = tr == tc
    col = lax.broadcasted_iota(I32, (tl, cap), 1)
    yrow = lax.broadcasted_iota(I32, (cap, PACK_W), 0)
    for u in range(s_ref.shape[0]):
        used = meta_ref[u, 2:3, 0:1] * CHUNK
        ys = ys_ref[u * cap:(u + 1) * cap, :]
        live = yrow < used
        ylo = jnp.where(live, _unpack2(ys, 0), 0.0).astype(BF16)
        yhi = jnp.where(live, _unpack2(ys, 1), 0.0).astype(BF16)
        s = s_ref[u]
        s1 = jnp.sum(jnp.where(eye, s[0:1, :], 0), axis=1, keepdims=True)
        s2 = jnp.sum(jnp.where(eye, s[1:2, :], 0), axis=1, keepdims=True)
        perm_t = ((col == s1) | (col == s2)).astype(BF16)
        half = PACK_W
        rows = slice(u * tl, (u + 1) * tl)
        out = jnp.concatenate(
            [x_ref[rows, :half] + jnp.dot(perm_t, ylo, preferred_element_type=F32),
             x_ref[rows, half:] + jnp.dot(perm_t, yhi, preferred_element_type=F32)], axis=1)
        if g_ref is not None:
            out = _rms(out, g_ref[...])
        o_ref[rows, :] = out


def _combine_norm(x2, ys, s12, meta, g_final):
    t, d = x2.shape
    tl = SORT_ROWS
    nt = t // tl
    ys2 = ys.reshape(ys.shape[0] * CHUNK, PACK_W)
    per = TILES_PER_STEP
    return pl.pallas_call(
        _combine_kernel,
        grid=(nt // per,),
        in_specs=[
            pl.BlockSpec((per * tl, d), lambda i: (i, 0)),
            pl.BlockSpec((per * SORT_CAP, PACK_W), lambda i: (i, 0)),
            pl.BlockSpec((per, SUBLANES, tl), lambda i: (i, 0, 0)),
            pl.BlockSpec((per, SUBLANES, LANES), lambda i: (i, 0, 0)),
            pl.BlockSpec((1, d), lambda i: (0, 0)),
        ],
        out_specs=pl.BlockSpec((per * tl, d), lambda i: (i, 0)),
        out_shape=jax.ShapeDtypeStruct((t, d), F32),
        compiler_params=pltpu.CompilerParams(
            dimension_semantics=("parallel",), vmem_limit_bytes=VMEM_LIMIT),
        name="combine",
    )(x2, ys2, s12, meta, g_final)


def _rope_tables(seq):
    pos = jnp.arange(seq, dtype=F32)
    inv_freq = ROPE_THETA ** (-jnp.arange(0, ROPE_DIM, 2, dtype=F32) / ROPE_DIM)
    ang = pos[:, None] * inv_freq[None, :]
    cos, sin = jnp.cos(ang), jnp.sin(ang)
    half = ROPE_DIM // 2
    ones = jnp.ones((seq, HEAD_DIM - ROPE_DIM), F32)
    zeros_h = jnp.zeros((seq, half), F32)
    zeros_r = jnp.zeros((seq, HEAD_DIM - ROPE_DIM), F32)
    cos_h = jnp.concatenate([cos, cos, ones], axis=1)
    sp_h = jnp.concatenate([zeros_h, sin, zeros_r], axis=1)
    sm_h = jnp.concatenate([-sin, zeros_h, zeros_r], axis=1)
    rep = LANES // HEAD_DIM
    return (jnp.tile(cos_h, (1, rep)), jnp.tile(sp_h, (1, rep)), jnp.tile(sm_h, (1, rep)))


def kernel(x, norm_mix, w_in, attn_sink, conv_w, norm_attn_out, norm_conv_out, w_out, norm_ffn,
           w_router_group, b_router_group, w_router_expert, b_router_expert,
           w_expert_gate, w_expert_up, w_expert_down, norm_final):
    batch, seq, d = x.shape
    depth = w_in.shape[0]
    t = batch * seq
    assert seq % MIX_ROWS == 0 and seq % FRONT_ROWS == 0 and t % SORT_ROWS == 0
    cos_t, sp_t, sm_t = _rope_tables(seq)
    x2 = x.reshape(t, d)
    w_in_b = w_in.astype(BF16)
    w_out_b = w_out.astype(BF16)
    w_router, b_router = _router_params(w_router_group, b_router_group,
                                        w_router_expert, b_router_expert)
    g_final = norm_final.reshape(1, d)

    ys = s12 = meta = None
    for l in range(depth):
        g_mix = norm_mix[l].reshape(1, d)
        if l == 0:
            q, k, v, gb, cu = _front(x2, g_mix, w_in_b[l], cos_t, sp_t, sm_t, seq)
        else:
            x2, q, k, v, gb, cu = _combine_front(x2, ys, s12, meta, g_mix, w_in_b[l],
                                                 cos_t, sp_t, sm_t, seq)
        x2, xs, s12, meta = _mix(x2, q, k, v, gb, cu, attn_sink[l], conv_w[l],
                                 norm_attn_out[l].reshape(1, -1), norm_conv_out[l].reshape(1, -1),
                                 w_out_b[l], norm_ffn[l].reshape(1, d), w_router[l], b_router[l],
                                 batch, seq)
        bend, csrc, cdst = _plan(meta, t)
        ys = _experts(xs, bend, csrc, cdst,
                      w_expert_gate, w_expert_up, w_expert_down, l, t)
    return _combine_norm(x2, ys, s12, meta, g_final).reshape(batch, seq, d)
```

```python
import functools

import jax
import jax.numpy as jnp
from jax import lax
from jax.experimental import pallas as pl
from jax.experimental.pallas import tpu as pltpu

F32 = jnp.float32
BF16 = jnp.bfloat16
U32 = jnp.uint32
I32 = jnp.int32

N_HEADS = 8
N_KV_HEADS = 2
HEAD_DIM = 64
ATTN_WIDTH = N_HEADS * HEAD_DIM
KV_WIDTH = N_KV_HEADS * HEAD_DIM
WINDOW = 128
QBLOCK = 128
ROPE_DIM = HEAD_DIM // 4
ROPE_THETA = 500000.0
CONV_WIDTH = 512
N_GROUPS = 4
PER_GROUP = 8
N_EXPERTS = N_GROUPS * PER_GROUP
D_EXPERT = 512
EPS = 1e-6
NEG_INF = -1e30

LANES = 128
SUBLANES = 8

FRONT_ROWS = 512
MIX_ROWS = 512
SORT_ROWS = 256
TILES_PER_STEP = 2
CHUNK = SUBLANES
BLOCK_ROWS = 512
BLOCK_CHUNKS = BLOCK_ROWS // CHUNK
SORT_CAP = ((2 * SORT_ROWS + N_EXPERTS * (CHUNK - 1)) // LANES + 1) * LANES
TILE_CHUNKS = SORT_CAP // CHUNK
PACK_W = 512
XS_W = PACK_W + LANES

VMEM_LIMIT = 56 * 1024 * 1024


def _rms(x, g):
    return x * lax.rsqrt(jnp.mean(x * x, axis=-1, keepdims=True) + EPS) * g


def _pack2(lo, hi):
    return pltpu.pack_elementwise([lo, hi], packed_dtype=BF16)


def _unpack2(w, index):
    return pltpu.unpack_elementwise(w, index=index, packed_dtype=BF16, unpacked_dtype=F32)


def _combine_front_kernel(x_ref, ys_ref, s_ref, meta_ref, g_ref, w_ref, cos_ref, sp_ref, sm_ref,
                          cw_ref, xo_ref, q_ref, k_ref, v_ref, yc_ref, edge_ref):
    _combine_tiles(x_ref, ys_ref, s_ref, meta_ref, xo_ref, None)
    _front_kernel(xo_ref, g_ref, w_ref, cos_ref, sp_ref, sm_ref, cw_ref,
                  q_ref, k_ref, v_ref, yc_ref, edge_ref)


def _front_kernel(x_ref, g_ref, w_ref, cos_ref, sp_ref, sm_ref, cw_ref,
                  q_ref, k_ref, v_ref, yc_ref, edge_ref):
    tm = x_ref.shape[0]
    h = _rms(x_ref[...], g_ref[...]).astype(BF16)
    cos = cos_ref[...]
    sp = sp_ref[...]
    sm = sm_ref[...]

    def proj(lo, hi):
        return jnp.dot(h, w_ref[:, lo:hi], preferred_element_type=F32)

    def rope(t):
        half = ROPE_DIM // 2
        return (t * cos + pltpu.roll(t, half, axis=1) * sp
                + pltpu.roll(t, LANES - half, axis=1) * sm)

    wide = 2 * LANES
    for c in range(ATTN_WIDTH // wide):
        q = proj(c * wide, (c + 1) * wide)
        for j in range(2):
            q_ref[:, c * wide + j * LANES:c * wide + (j + 1) * LANES] = (
                rope(q[:, j * LANES:(j + 1) * LANES]) * (HEAD_DIM ** -0.5)).astype(BF16)
    o = ATTN_WIDTH
    kv = proj(o, o + 2 * KV_WIDTH)
    k_ref[...] = rope(kv[:, :KV_WIDTH]).astype(BF16)
    v_ref[...] = kv[:, KV_WIDTH:].astype(BF16)
    o += 2 * KV_WIDTH
    gb = proj(o, o + CONV_WIDTH)
    o += CONV_WIDTH
    gc = proj(o, o + CONV_WIDTH)
    o += CONV_WIDTH
    u = proj(o, o + CONV_WIDTH)
    cu = gc * u
    ridx = lax.broadcasted_iota(I32, cu.shape, 0)
    up = jnp.where(ridx == 0, 0.0, pltpu.roll(cu, 1, axis=0))
    dn = jnp.where(ridx == tm - 1, 0.0, pltpu.roll(cu, tm - 1, axis=0))
    conv = up * cw_ref[0:1, :] + cu * cw_ref[1:2, :] + dn * cw_ref[2:3, :]
    yc_ref[...] = (gb * conv).astype(BF16)
    edge_ref[0] = jnp.concatenate(
        [cu[0:1], cu[tm - 1:tm], gb[0:1], gb[tm - 1:tm],
         jnp.zeros((SUBLANES - 4, CONV_WIDTH), F32)], axis=0)


def _combine_front(x2, ys, s12, meta, g, w_in_b, cos_t, sp_t, sm_t, conv_w, seq):
    t, d = x2.shape
    tm = FRONT_ROWS
    per = tm // SORT_ROWS
    nseq = seq // tm
    row = lambda i: (i, 0)
    tile = lambda i: (i, 0, 0)
    pos = lambda i: (i % nseq, 0)
    fixed = lambda i: (0, 0)
    ys2 = ys.reshape(ys.shape[0] * CHUNK, PACK_W)
    return pl.pallas_call(
        _combine_front_kernel,
        grid=(t // tm,),
        in_specs=[
            pl.BlockSpec((tm, d), row),
            pl.BlockSpec((per * SORT_CAP, PACK_W), row),
            pl.BlockSpec((per, SUBLANES, SORT_ROWS), tile),
            pl.BlockSpec((per, SUBLANES, LANES), tile),
            pl.BlockSpec((1, d), fixed),
            pl.BlockSpec(w_in_b.shape, fixed),
            pl.BlockSpec((tm, LANES), pos),
            pl.BlockSpec((tm, LANES), pos),
            pl.BlockSpec((tm, LANES), pos),
            pl.BlockSpec(conv_w.shape, fixed),
        ],
        out_specs=[
            pl.BlockSpec((tm, d), row),
            pl.BlockSpec((tm, ATTN_WIDTH), row),
            pl.BlockSpec((tm, KV_WIDTH), row),
            pl.BlockSpec((tm, KV_WIDTH), row),
            pl.BlockSpec((tm, CONV_WIDTH), row),
            pl.BlockSpec((1, SUBLANES, CONV_WIDTH), lambda i: (i, 0, 0)),
        ],
        out_shape=[
            jax.ShapeDtypeStruct((t, d), F32),
            jax.ShapeDtypeStruct((t, ATTN_WIDTH), BF16),
            jax.ShapeDtypeStruct((t, KV_WIDTH), BF16),
            jax.ShapeDtypeStruct((t, KV_WIDTH), BF16),
            jax.ShapeDtypeStruct((t, CONV_WIDTH), BF16),
            jax.ShapeDtypeStruct((t // tm, SUBLANES, CONV_WIDTH), F32),
        ],
        compiler_params=pltpu.CompilerParams(
            dimension_semantics=("parallel",), vmem_limit_bytes=VMEM_LIMIT),
        name="combine_front",
    )(x2, ys2, s12, meta, g, w_in_b, cos_t, sp_t, sm_t, conv_w)


def _front(x2, g, w_in_b, cos_t, sp_t, sm_t, conv_w, seq):
    t, d = x2.shape
    tm = FRONT_ROWS
    nseq = seq // tm
    row = lambda i: (i, 0)
    pos = lambda i: (i % nseq, 0)
    fixed = lambda i: (0, 0)
    return pl.pallas_call(
        _front_kernel,
        grid=(t // tm,),
        in_specs=[
            pl.BlockSpec((tm, d), row),
            pl.BlockSpec((1, d), fixed),
            pl.BlockSpec(w_in_b.shape, fixed),
            pl.BlockSpec((tm, LANES), pos),
            pl.BlockSpec((tm, LANES), pos),
            pl.BlockSpec((tm, LANES), pos),
            pl.BlockSpec(conv_w.shape, fixed),
        ],
        out_specs=[
            pl.BlockSpec((tm, ATTN_WIDTH), row),
            pl.BlockSpec((tm, KV_WIDTH), row),
            pl.BlockSpec((tm, KV_WIDTH), row),
            pl.BlockSpec((tm, CONV_WIDTH), row),
            pl.BlockSpec((1, SUBLANES, CONV_WIDTH), lambda i: (i, 0, 0)),
        ],
        out_shape=[
            jax.ShapeDtypeStruct((t, ATTN_WIDTH), BF16),
            jax.ShapeDtypeStruct((t, KV_WIDTH), BF16),
            jax.ShapeDtypeStruct((t, KV_WIDTH), BF16),
            jax.ShapeDtypeStruct((t, CONV_WIDTH), BF16),
            jax.ShapeDtypeStruct((t // tm, SUBLANES, CONV_WIDTH), F32),
        ],
        compiler_params=pltpu.CompilerParams(
            dimension_semantics=("parallel",), vmem_limit_bytes=VMEM_LIMIT),
        name="front",
    )(x2, g, w_in_b, cos_t, sp_t, sm_t, conv_w)


def _mix_kernel(sink_ref, x_ref, q_ref, kp_ref, kc_ref, kn_ref, vp_ref, vc_ref, vn_ref,
                yc_ref, edge_ref, edgep_ref, edgen_ref, cw_ref, ga_ref, gc_ref, wo_ref, bias_ref,
                gf_ref, wr_ref, br_ref,
                o_ref, xs_ref, s_ref, meta_ref, ya_ref, kd_ref, vd_ref, *, seq):
    tq = x_ref.shape[0]
    i = pl.program_id(1)
    nseq = pl.num_programs(1)
    grp = N_HEADS // N_KV_HEADS
    rows = grp * QBLOCK
    band = 3 * QBLOCK

    kcat = jnp.concatenate([kp_ref[...], kc_ref[...], kn_ref[...]], axis=0)
    vcat = jnp.concatenate([vp_ref[...], vc_ref[...], vn_ref[...]], axis=0)
    ones = jnp.ones((kcat.shape[0], HEAD_DIM), BF16)
    for h in range(N_KV_HEADS):
        kh = kcat[:, h * HEAD_DIM:(h + 1) * HEAD_DIM]
        vh = vcat[:, h * HEAD_DIM:(h + 1) * HEAD_DIM]
        kd_ref[h] = jnp.concatenate([kh, kh], axis=1)
        vd_ref[2 * h] = jnp.concatenate([vh, ones], axis=1)
        vd_ref[2 * h + 1] = jnp.concatenate([ones, vh], axis=1)

    lane = lax.broadcasted_iota(I32, (QBLOCK, LANES), 1)
    low = lane < HEAD_DIM
    low2 = lax.broadcasted_iota(I32, (2 * QBLOCK, LANES), 1) < HEAD_DIM
    hrow = lax.broadcasted_iota(I32, (rows, 1), 0) // QBLOCK
    head_order = (0, 2, 1, 3)
    sinks = []
    for hk in range(N_KV_HEADS):
        sink = jnp.zeros((rows, 1), F32)
        for g in range(grp):
            sink = jnp.where(hrow == g, sink_ref[hk * grp + head_order[g]], sink)
        sinks.append(sink)

    def qblock(jb, carry):
        r0 = pl.multiple_of(jb * QBLOCK, QBLOCK)
        pos0 = i * tq + r0
        bias_prev = bias_ref[jnp.where(pos0 == 0, 1, 0)]
        bias_next = bias_ref[jnp.where(pos0 + QBLOCK == seq, 3, 2)]
        qb = q_ref[pl.ds(r0, QBLOCK), :]
        for hk in range(N_KV_HEADS):
            q2 = [qb[:, (2 * hk + cc) * LANES:(2 * hk + cc + 1) * LANES] for cc in range(2)]
            qs = jnp.concatenate(
                [jnp.where(low, q, jnp.zeros_like(q)) for q in q2]
                + [jnp.where(low, jnp.zeros_like(q), q) for q in q2], axis=0)
            sink = sinks[hk]
            s = lax.dot_general(qs, kd_ref[hk, pl.ds(r0, band), :], (((1,), (1,)), ((), ())),
                                preferred_element_type=F32)
            s = jnp.concatenate([s[:, :QBLOCK] + jnp.concatenate([bias_prev] * grp, axis=0),
                                 s[:, QBLOCK:2 * QBLOCK],
                                 s[:, 2 * QBLOCK:] + jnp.concatenate([bias_next] * grp, axis=0)],
                                axis=1)
            m = jnp.maximum(jnp.max(s, axis=1, keepdims=True), sink)
            p = jnp.exp((s - m).astype(BF16))
            sink_p = jnp.exp(sink - m)
            half_rows = 2 * QBLOCK
            o_lo = jnp.dot(p[:half_rows], vd_ref[2 * hk, pl.ds(r0, band), :],
                           preferred_element_type=F32)
            o_hi = jnp.dot(p[half_rows:], vd_ref[2 * hk + 1, pl.ds(r0, band), :],
                           preferred_element_type=F32)
            den_lo = jnp.where(low2, pltpu.roll(o_lo, HEAD_DIM, axis=1), o_lo) + sink_p[:half_rows]
            den_hi = jnp.where(low2, o_hi, pltpu.roll(o_hi, HEAD_DIM, axis=1)) + sink_p[half_rows:]
            y_lo = o_lo * (1.0 / den_lo)
            y_hi = o_hi * (1.0 / den_hi)
            for cc in range(2):
                c = 2 * hk + cc
                ya_ref[pl.ds(r0, QBLOCK), c * LANES:(c + 1) * LANES] = jnp.where(
                    low, y_lo[cc * QBLOCK:(cc + 1) * QBLOCK], y_hi[cc * QBLOCK:(cc + 1) * QBLOCK])
        return carry

    for jb in range(tq // QBLOCK):
        qblock(jb, 0)

    own, prev, nxt = edge_ref[0], edgep_ref[0], edgen_ref[0]
    first = jnp.where(i > 0, own[2:3] * cw_ref[0:1, :] * prev[1:2], 0.0)
    last = jnp.where(i < nseq - 1, own[3:4] * cw_ref[2:3, :] * nxt[0:1], 0.0)
    eg = 2 * SUBLANES
    re = lax.broadcasted_iota(I32, (eg, CONV_WIDTH), 0)
    yc = jnp.concatenate(
        [yc_ref[0:eg, :].astype(F32) + jnp.where(re == 0, first, 0.0),
         yc_ref[eg:tq - eg, :].astype(F32),
         yc_ref[tq - eg:tq, :].astype(F32) + jnp.where(re == eg - 1, last, 0.0)],
        axis=0)
    yc = _rms(yc, gc_ref[...]).astype(BF16)
    ya = _rms(ya_ref[...], ga_ref[...]).astype(BF16)
    o_ref[...] = (x_ref[...]
                  + jnp.dot(ya, wo_ref[:ATTN_WIDTH, :], preferred_element_type=F32)
                  + jnp.dot(yc, wo_ref[ATTN_WIDTH:, :], preferred_element_type=F32))
    for u in range(xs_ref.shape[0]):
        _route_tile(u, o_ref, gf_ref, wr_ref, br_ref, xs_ref, s_ref, meta_ref)


def _mix(x2, q, k, v, yc, edge, sink, conv_w, g_attn, g_conv, w_out_b, g_ffn, w_router, b_router,
         batch, seq):
    t, d = x2.shape
    tq = MIX_ROWS
    assert tq == FRONT_ROWS
    nseq = seq // tq
    per = tq // SORT_ROWS
    nt = t // SORT_ROWS
    tile = lambda b, i: (b * nseq + i, 0, 0)
    tprev = lambda b, i: (b * nseq + jnp.maximum(i - 1, 0), 0, 0)
    tnext = lambda b, i: (b * nseq + jnp.minimum(i + 1, nseq - 1), 0, 0)
    kb = tq // QBLOCK
    nkb = seq // QBLOCK

    row = lambda b, i: (b * nseq + i, 0)
    kprev = lambda b, i: (b * nkb + jnp.maximum(i * kb - 1, 0), 0)
    knext = lambda b, i: (b * nkb + jnp.minimum((i + 1) * kb, nkb - 1), 0)
    fixed = lambda b, i: (0, 0)
    edge_block = (1, SUBLANES, CONV_WIDTH)
    qi = lax.broadcasted_iota(I32, (QBLOCK, QBLOCK), 0)
    kj = lax.broadcasted_iota(I32, (QBLOCK, QBLOCK), 1)
    neg = jnp.full((QBLOCK, QBLOCK), NEG_INF, F32)
    band_bias = jnp.stack([jnp.where(kj - QBLOCK - qi >= -WINDOW, 0.0, NEG_INF).astype(F32), neg,
                           jnp.where(kj + QBLOCK - qi <= WINDOW, 0.0, NEG_INF).astype(F32), neg])
    return pl.pallas_call(
        functools.partial(_mix_kernel, seq=seq),
        grid=(batch, nseq),
        in_specs=[
            pl.BlockSpec(memory_space=pltpu.SMEM),
            pl.BlockSpec((tq, d), row),
            pl.BlockSpec((tq, ATTN_WIDTH), row),
            pl.BlockSpec((QBLOCK, KV_WIDTH), kprev),
            pl.BlockSpec((tq, KV_WIDTH), row),
            pl.BlockSpec((QBLOCK, KV_WIDTH), knext),
            pl.BlockSpec((QBLOCK, KV_WIDTH), kprev),
            pl.BlockSpec((tq, KV_WIDTH), row),
            pl.BlockSpec((QBLOCK, KV_WIDTH), knext),
            pl.BlockSpec((tq, CONV_WIDTH), row),
            pl.BlockSpec(edge_block, tile),
            pl.BlockSpec(edge_block, tprev),
            pl.BlockSpec(edge_block, tnext),
            pl.BlockSpec(conv_w.shape, fixed),
            pl.BlockSpec((1, ATTN_WIDTH), fixed),
            pl.BlockSpec((1, CONV_WIDTH), fixed),
            pl.BlockSpec(w_out_b.shape, fixed),
            pl.BlockSpec(band_bias.shape, lambda b, i: (0, 0, 0)),
            pl.BlockSpec((1, d), fixed),
            pl.BlockSpec(w_router.shape, fixed),
            pl.BlockSpec(b_router.shape, fixed),
        ],
        out_specs=[
            pl.BlockSpec((tq, d), row),
            pl.BlockSpec((per, SORT_CAP, XS_W), tile),
            pl.BlockSpec((per, SUBLANES, SORT_ROWS), tile),
            pl.BlockSpec((per, SUBLANES, LANES), tile),
        ],
        out_shape=[
            jax.ShapeDtypeStruct((t, d), F32),
            jax.ShapeDtypeStruct((nt, SORT_CAP, XS_W), U32),
            jax.ShapeDtypeStruct((nt, SUBLANES, SORT_ROWS), I32),
            jax.ShapeDtypeStruct((nt, SUBLANES, LANES), I32),
        ],
        scratch_shapes=[pltpu.VMEM((tq, ATTN_WIDTH), F32),
                        pltpu.VMEM((N_KV_HEADS, tq + 2 * QBLOCK, LANES), BF16),
                        pltpu.VMEM((2 * N_KV_HEADS, tq + 2 * QBLOCK, LANES), BF16)],
        compiler_params=pltpu.CompilerParams(
            dimension_semantics=("parallel", "parallel"), vmem_limit_bytes=VMEM_LIMIT),
        name="mix",
    )(sink, x2, q, k, k, k, v, v, v, yc, edge, edge, edge, conv_w, g_attn, g_conv, w_out_b, band_bias,
      g_ffn, w_router, b_router)


ROUTER_ROWS = N_EXPERTS + SUBLANES


def _route_tile(u, x_ref, g_ref, w_ref, b_ref, xs_ref, s_ref, meta_ref):
    tl = SORT_ROWS
    h = _rms(x_ref[u * tl:(u + 1) * tl, :], g_ref[...])
    hb = h.astype(BF16)

    dn = (((1,), (1,)), ((), ()))
    h2 = (h - hb.astype(F32)).astype(BF16)
    both = lax.dot_general(w_ref[...], hb, dn, preferred_element_type=F32)
    corr = lax.dot_general(w_ref[:ROUTER_ROWS, :], h2, dn, preferred_element_type=F32)
    logits = both[:ROUTER_ROWS] + both[ROUTER_ROWS:] + corr + b_ref[...]
    el = logits[:N_EXPERTS]
    gl = logits[N_EXPERTS:]

    r8 = lax.broadcasted_iota(I32, gl.shape, 0)
    glm = jnp.where(r8 < N_GROUPS, gl, -jnp.inf)
    gmax = jnp.max(glm, axis=0, keepdims=True)
    grp = jnp.min(jnp.where(glm == gmax, r8, SUBLANES), axis=0, keepdims=True)
    p_grp = 1.0 / jnp.sum(jnp.exp(glm - gmax), axis=0, keepdims=True)

    r32 = lax.broadcasted_iota(I32, el.shape, 0)
    e0 = jnp.where((r32 // PER_GROUP) == grp, el, -jnp.inf)
    t1 = jnp.max(e0, axis=0, keepdims=True)
    i1 = jnp.min(jnp.where(e0 == t1, r32, N_EXPERTS), axis=0, keepdims=True)
    a1 = r32 == i1
    e1 = jnp.where(a1, -jnp.inf, e0)
    t2 = jnp.max(e1, axis=0, keepdims=True)
    i2 = jnp.min(jnp.where(e1 == t2, r32, N_EXPERTS), axis=0, keepdims=True)
    a2 = r32 == i2
    ed = jnp.exp(t2 - t1)
    inv = 1.0 / (1.0 + ed)
    g1 = p_grp * inv
    g2 = p_grp * (ed * inv)

    af = (a1 | a2).astype(F32)
    cnt = jnp.sum(af, axis=1, keepdims=True)
    pc = jnp.floor((cnt + (CHUNK - 1)) * (1.0 / CHUNK)) * CHUNK
    er = lax.broadcasted_iota(I32, (N_EXPERTS, LANES), 0)
    ec = lax.broadcasted_iota(I32, (N_EXPERTS, LANES), 1)
    pc_row = jnp.sum(jnp.where(er == ec, pc, 0.0), axis=0, keepdims=True)
    lo_row = jnp.sum(jnp.where(er < ec, pc, 0.0), axis=0, keepdims=True)
    lo_col = jnp.sum(jnp.where(ec < er, pc_row, 0.0), axis=1, keepdims=True)
    total = jnp.sum(pc, axis=0, keepdims=True)

    tr = lax.broadcasted_iota(I32, (tl, tl), 0)
    tc = lax.broadcasted_iota(I32, (tl, tl), 1)
    upper = (tr <= tc).astype(BF16)
    incl = jnp.dot(af.astype(BF16), upper, preferred_element_type=F32)
    slot = lo_col + incl - af
    s1 = jnp.sum(jnp.where(a1, slot, 0.0), axis=0, keepdims=True).astype(I32)
    s2 = jnp.sum(jnp.where(a2, slot, 0.0), axis=0, keepdims=True).astype(I32)

    cap = xs_ref.shape[1]
    srow = lax.broadcasted_iota(I32, (cap, tl), 0)
    p1 = srow == s1
    p2 = srow == s2
    perm = (p1 | p2).astype(BF16)
    half = hb.shape[1] // 2
    lo = jnp.dot(perm, hb[:, :half], preferred_element_type=F32)
    hi = jnp.dot(perm, hb[:, half:], preferred_element_type=F32)
    xs_ref[u, :, :PACK_W] = _pack2(lo, hi)
    gs = jnp.sum(jnp.where(p1, g1, 0.0) + jnp.where(p2, g2, 0.0), axis=1, keepdims=True)
    gs = jnp.broadcast_to(gs, (cap, LANES))
    gs_hi = gs.astype(BF16).astype(F32)
    xs_ref[u, :, PACK_W:] = _pack2(gs_hi, gs - gs_hi)

    sr = lax.broadcasted_iota(I32, (SUBLANES, tl), 0)
    s_ref[u] = jnp.where(sr == 0, s1, jnp.where(sr == 1, s2, 0))
    mr = lax.broadcasted_iota(I32, (SUBLANES, LANES), 0)
    meta = jnp.where(mr == 0, pc_row, jnp.where(mr == 1, lo_row, jnp.where(mr == 2, total, 0.0)))
    meta_ref[u] = (meta * (1.0 / CHUNK)).astype(I32)


def _router_params(w_group, b_group, w_expert, b_expert):
    pad = SUBLANES - N_GROUPS
    w = jnp.concatenate([jnp.swapaxes(w_expert, 1, 2),
                         jnp.pad(jnp.swapaxes(w_group, 1, 2), ((0, 0), (0, pad), (0, 0)))], axis=1)
    w1 = w.astype(BF16)
    w2 = (w - w1.astype(F32)).astype(BF16)
    b = jnp.concatenate([b_expert, jnp.pad(b_group, ((0, 0), (0, pad)))], axis=1)[:, :, None]
    return jnp.concatenate([w1, w2], axis=1), b


def _n_blocks(t):
    nt = t // SORT_ROWS
    max_chunks = (2 * t + (CHUNK - 1) * N_EXPERTS * nt) // CHUNK + (BLOCK_CHUNKS - 1) * N_EXPERTS
    return -(-max_chunks // BLOCK_CHUNKS)


PLAN_LANES = 1024


def _list_len(t):
    return -(-(_n_blocks(t) + 1) * BLOCK_CHUNKS // PLAN_LANES) * PLAN_LANES


def _plan_kernel(nch_ref, ncht_ref, lo8_ref, csrc_ref, cdst_ref, bend_ref):
    nt = nch_ref.shape[0]
    ch = csrc_ref.shape[1]
    nch = nch_ref[:, :N_EXPERTS].astype(F32)
    lo8 = lo8_ref[:, :N_EXPERTS].astype(F32)
    ncht = ncht_ref[...].astype(F32)

    tc = jnp.sum(ncht, axis=1, keepdims=True)
    ptc = jnp.floor((tc + (BLOCK_CHUNKS - 1)) * (1.0 / BLOCK_CHUNKS)) * BLOCK_CHUNKS
    er = lax.broadcasted_iota(I32, (N_EXPERTS, LANES), 0)
    ec = lax.broadcasted_iota(I32, (N_EXPERTS, LANES), 1)
    ptc_row = jnp.sum(jnp.where(er == ec, ptc, 0.0), axis=0, keepdims=True)
    eend = jnp.sum(jnp.where(ec <= er, ptc_row, 0.0), axis=1, keepdims=True)
    estart = eend - ptc
    total = jnp.sum(ptc, axis=0, keepdims=True)

    tr = lax.broadcasted_iota(I32, (nt, nt), 0)
    tcol = lax.broadcasted_iota(I32, (nt, nt), 1)
    lower = (tcol <= tr).astype(BF16)
    cum = jnp.dot(lower, nch.astype(BF16), preferred_element_type=F32)
    cum_hi = jnp.floor(cum * (1.0 / LANES))
    cum_lo = cum - cum_hi * LANES

    k = (pl.program_id(0) * ch + lax.broadcasted_iota(I32, (1, ch), 1)).astype(F32)
    r32 = lax.broadcasted_iota(I32, (N_EXPERTS, ch), 0).astype(F32)

    def expert_of(pos):
        return jnp.minimum(jnp.sum((eend <= pos).astype(F32), axis=0, keepdims=True),
                           N_EXPERTS - 1.0)

    e_k = expert_of(k)
    sel_e = r32 == e_k
    oh_e = sel_e.astype(BF16)
    start_k = jnp.sum(jnp.where(sel_e, estart, 0.0), axis=0, keepdims=True)
    tc_k = jnp.sum(jnp.where(sel_e, tc, 0.0), axis=0, keepdims=True)
    q = k - start_k

    def pick(tbl):
        return jnp.dot(tbl.astype(BF16), oh_e, preferred_element_type=F32)

    cum_e = pick(cum_hi) * LANES + pick(cum_lo)
    nch_e = pick(nch)
    lo8_e = pick(lo8)
    rnt = lax.broadcasted_iota(I32, (nt, ch), 0).astype(F32)
    i_k = jnp.minimum(jnp.sum((cum_e <= q).astype(F32), axis=0, keepdims=True), nt - 1.0)
    sel_i = rnt == i_k
    before = jnp.sum(jnp.where(sel_i, cum_e - nch_e, 0.0), axis=0, keepdims=True)
    lo8_k = jnp.sum(jnp.where(sel_i, lo8_e, 0.0), axis=0, keepdims=True)
    src = i_k * TILE_CHUNKS + lo8_k + (q - before)
    pad = (q >= tc_k) | (k >= total)
    csrc_ref[...] = jnp.where(pad, TILE_CHUNKS - 1.0, src).astype(I32)
    cdst_ref[...] = jnp.where(pad, nt * TILE_CHUNKS + k, src).astype(I32)

    eend_row = jnp.sum(jnp.where(er <= ec, ptc, 0.0), axis=0, keepdims=True)
    bend_ref[...] = (eend_row * (1.0 / BLOCK_CHUNKS)).astype(I32)


def _plan(meta, t):
    nt = t // SORT_ROWS
    nl = _list_len(t)
    nch = meta[:, 0, :]
    lo8 = meta[:, 1, :]
    ncht = nch[:, :N_EXPERTS].T
    fixed = lambda c: (0, 0)
    csrc, cdst, bend = pl.pallas_call(
        _plan_kernel,
        grid=(nl // PLAN_LANES,),
        in_specs=[
            pl.BlockSpec(nch.shape, fixed),
            pl.BlockSpec(ncht.shape, fixed),
            pl.BlockSpec(lo8.shape, fixed),
        ],
        out_specs=[
            pl.BlockSpec((1, PLAN_LANES), lambda c: (0, c)),
            pl.BlockSpec((1, PLAN_LANES), lambda c: (0, c)),
            pl.BlockSpec((1, LANES), fixed),
        ],
        out_shape=[
            jax.ShapeDtypeStruct((1, nl), I32),
            jax.ShapeDtypeStruct((1, nl), I32),
            jax.ShapeDtypeStruct((1, LANES), I32),
        ],
        compiler_params=pltpu.CompilerParams(dimension_semantics=("arbitrary",)),
        name="plan",
    )(nch, ncht, lo8)
    return bend.reshape(-1), csrc.reshape(-1), cdst.reshape(-1)


def _expert_kernel(bend_ref, csrc_ref, cdst_ref,
                   xs_hbm, wg_ref, wu_ref, wd_ref, ys_hbm,
                   xbuf, ybuf, wgb, wub, wdb, sem_in, sem_out):
    e = pl.program_id(0)
    n_used = bend_ref[N_EXPERTS - 1]
    b_lo = jnp.where(e == 0, 0, bend_ref[jnp.maximum(e - 1, 0)])
    b_hi = bend_ref[e]

    def gather(blk, sl):
        return [pltpu.make_async_copy(xs_hbm.at[csrc_ref[blk * BLOCK_CHUNKS + j]],
                                      xbuf.at[sl, j], sem_in.at[sl])
                for j in range(BLOCK_CHUNKS)]

    def scatter(blk, sl):
        return [pltpu.make_async_copy(ybuf.at[sl, j],
                                      ys_hbm.at[cdst_ref[blk * BLOCK_CHUNKS + j]], sem_out.at[sl])
                for j in range(BLOCK_CHUNKS)]

    @pl.when(e == 0)
    def _():
        for c in gather(0, 0):
            c.start()

    @pl.when(b_hi > b_lo)
    def _():
        wgb[...] = wg_ref[0, 0].astype(BF16)
        wub[...] = wu_ref[0, 0].astype(BF16)
        wdb[...] = wd_ref[0, 0].astype(BF16)

    def block(b, carry):
        slot = lax.rem(b, 2)
        for c in gather(b, slot):
            c.wait()
        for c in gather(b + 1, 1 - slot):
            c.start()
        w = xbuf[slot].reshape(BLOCK_ROWS, XS_W)
        xw = w[:, :PACK_W]
        lo = _unpack2(xw, 0).astype(BF16)
        hi = _unpack2(xw, 1).astype(BF16)
        gw = w[:, PACK_W:]
        gate = (_unpack2(gw, 0) + _unpack2(gw, 1))[:, 0:1]
        half = PACK_W
        hg = (jnp.dot(lo, wgb[:half, :], preferred_element_type=F32)
              + jnp.dot(hi, wgb[half:, :], preferred_element_type=F32))
        hu = (jnp.dot(lo, wub[:half, :], preferred_element_type=F32)
              + jnp.dot(hi, wub[half:, :], preferred_element_type=F32))
        act = (hg * (1.0 / (1.0 + jnp.exp(-hg))) * hu).astype(BF16)
        y = jnp.dot(act, wdb[...], preferred_element_type=F32) * gate
        ybuf[slot] = _pack2(y[:, :half], y[:, half:]).reshape(BLOCK_CHUNKS, CHUNK, PACK_W)
        for c in scatter(b, slot):
            c.start(priority=1)

        @pl.when(b >= 1)
        def _():
            for c in scatter(b - 1, 1 - slot):
                c.wait()

        return carry

    lax.fori_loop(b_lo, b_hi, block, 0)

    @pl.when(e == N_EXPERTS - 1)
    def _():
        last = n_used - 1
        for c in scatter(last, lax.rem(last, 2)):
            c.wait()
        for c in gather(n_used, lax.rem(n_used, 2)):
            c.wait()


def _experts(xs, bend, csrc, cdst, w_gate, w_up, w_down, layer, t):
    nt = t // SORT_ROWS
    d = w_gate.shape[2]
    xs3 = xs.reshape(nt * TILE_CHUNKS, CHUNK, XS_W)
    n_out = nt * TILE_CHUNKS + _list_len(t)
    wmap = lambda e, be, cs, cd: (layer, e, 0, 0)
    grid_spec = pltpu.PrefetchScalarGridSpec(
        num_scalar_prefetch=3,
        grid=(N_EXPERTS,),
        in_specs=[
            pl.BlockSpec(memory_space=pl.ANY),
            pl.BlockSpec((1, 1, d, D_EXPERT), wmap),
            pl.BlockSpec((1, 1, d, D_EXPERT), wmap),
            pl.BlockSpec((1, 1, D_EXPERT, d), wmap),
        ],
        out_specs=pl.BlockSpec(memory_space=pl.ANY),
        scratch_shapes=[
            pltpu.VMEM((2, BLOCK_CHUNKS, CHUNK, XS_W), U32),
            pltpu.VMEM((2, BLOCK_CHUNKS, CHUNK, PACK_W), U32),
            pltpu.VMEM((d, D_EXPERT), BF16),
            pltpu.VMEM((d, D_EXPERT), BF16),
            pltpu.VMEM((D_EXPERT, d), BF16),
            pltpu.SemaphoreType.DMA((2,)),
            pltpu.SemaphoreType.DMA((2,)),
        ],
    )
    return pl.pallas_call(
        _expert_kernel,
        grid_spec=grid_spec,
        out_shape=jax.ShapeDtypeStruct((n_out, CHUNK, PACK_W), U32),
        compiler_params=pltpu.CompilerParams(
            dimension_semantics=("arbitrary",), vmem_limit_bytes=VMEM_LIMIT),
        name="experts",
    )(bend, csrc, cdst, xs3, w_gate, w_up, w_down)


def _combine_kernel(x_ref, ys_ref, s_ref, meta_ref, g_ref, o_ref):
    _combine_tiles(x_ref, ys_ref, s_ref, meta_ref, o_ref, g_ref)


def _combine_tiles(x_ref, ys_ref, s_ref, meta_ref, o_ref, g_ref):
    tl = SORT_ROWS
    cap = SORT_CAP
    tr = lax.broadcasted_iota(I32, (tl, tl), 0)
    tc = lax.broadcasted_iota(I32, (tl, tl), 1)
    eye = tr == tc
    col = lax.broadcasted_iota(I32, (tl, cap), 1)
    yrow = lax.broadcasted_iota(I32, (cap, PACK_W), 0)
    for u in range(s_ref.shape[0]):
        used = meta_ref[u, 2:3, 0:1] * CHUNK
        ys = ys_ref[u * cap:(u + 1) * cap, :]
        live = yrow < used
        ylo = jnp.where(live, _unpack2(ys, 0), 0.0).astype(BF16)
        yhi = jnp.where(live, _unpack2(ys, 1), 0.0).astype(BF16)
        s = s_ref[u]
        s1 = jnp.sum(jnp.where(eye, s[0:1, :], 0), axis=1, keepdims=True)
        s2 = jnp.sum(jnp.where(eye, s[1:2, :], 0), axis=1, keepdims=True)
        perm_t = ((col == s1) | (col == s2)).astype(BF16)
        half = PACK_W
        rows = slice(u * tl, (u + 1) * tl)
        out = jnp.concatenate(
            [x_ref[rows, :half] + jnp.dot(perm_t, ylo, preferred_element_type=F32),
             x_ref[rows, half:] + jnp.dot(perm_t, yhi, preferred_element_type=F32)], axis=1)
        if g_ref is not None:
            out = _rms(out, g_ref[...])
        o_ref[rows, :] = out


def _combine_norm(x2, ys, s12, meta, g_final):
    t, d = x2.shape
    tl = SORT_ROWS
    nt = t // tl
    ys2 = ys.reshape(ys.shape[0] * CHUNK, PACK_W)
    per = TILES_PER_STEP
    return pl.pallas_call(
        _combine_kernel,
        grid=(nt // per,),
        in_specs=[
            pl.BlockSpec((per * tl, d), lambda i: (i, 0)),
            pl.BlockSpec((per * SORT_CAP, PACK_W), lambda i: (i, 0)),
            pl.BlockSpec((per, SUBLANES, tl), lambda i: (i, 0, 0)),
            pl.BlockSpec((per, SUBLANES, LANES), lambda i: (i, 0, 0)),
            pl.BlockSpec((1, d), lambda i: (0, 0)),
        ],
        out_specs=pl.BlockSpec((per * tl, d), lambda i: (i, 0)),
        out_shape=jax.ShapeDtypeStruct((t, d), F32),
        compiler_params=pltpu.CompilerParams(
            dimension_semantics=("parallel",), vmem_limit_bytes=VMEM_LIMIT),
        name="combine",
    )(x2, ys2, s12, meta, g_final)


def _rope_tables(seq):
    pos = jnp.arange(seq, dtype=F32)
    inv_freq = ROPE_THETA ** (-jnp.arange(0, ROPE_DIM, 2, dtype=F32) / ROPE_DIM)
    ang = pos[:, None] * inv_freq[None, :]
    cos, sin = jnp.cos(ang), jnp.sin(ang)
    half = ROPE_DIM // 2
    ones = jnp.ones((seq, HEAD_DIM - ROPE_DIM), F32)
    zeros_h = jnp.zeros((seq, half), F32)
    zeros_r = jnp.zeros((seq, HEAD_DIM - ROPE_DIM), F32)
    cos_h = jnp.concatenate([cos, cos, ones], axis=1)
    sp_h = jnp.concatenate([zeros_h, sin, zeros_r], axis=1)
    sm_h = jnp.concatenate([-sin, zeros_h, zeros_r], axis=1)
    rep = LANES // HEAD_DIM
    return (jnp.tile(cos_h, (1, rep)), jnp.tile(sp_h, (1, rep)), jnp.tile(sm_h, (1, rep)))


def kernel(x, norm_mix, w_in, attn_sink, conv_w, norm_attn_out, norm_conv_out, w_out, norm_ffn,
           w_router_group, b_router_group, w_router_expert, b_router_expert,
           w_expert_gate, w_expert_up, w_expert_down, norm_final):
    batch, seq, d = x.shape
    depth = w_in.shape[0]
    t = batch * seq
    assert seq % MIX_ROWS == 0 and seq % FRONT_ROWS == 0 and t % SORT_ROWS == 0
    cos_t, sp_t, sm_t = _rope_tables(seq)
    x2 = x.reshape(t, d)
    w_in_b = w_in.astype(BF16)
    w_out_b = w_out.astype(BF16)
    w_router, b_router = _router_params(w_router_group, b_router_group,
                                        w_router_expert, b_router_expert)
    g_final = norm_final.reshape(1, d)

    ys = s12 = meta = None
    for l in range(depth):
        g_mix = norm_mix[l].reshape(1, d)
        if l == 0:
            q, k, v, yc, edge = _front(x2, g_mix, w_in_b[l], cos_t, sp_t, sm_t, conv_w[l], seq)
        else:
            x2, q, k, v, yc, edge = _combine_front(x2, ys, s12, meta, g_mix, w_in_b[l],
                                                   cos_t, sp_t, sm_t, conv_w[l], seq)
        x2, xs, s12, meta = _mix(x2, q, k, v, yc, edge, attn_sink[l], conv_w[l],
                                 norm_attn_out[l].reshape(1, -1), norm_conv_out[l].reshape(1, -1),
                                 w_out_b[l], norm_ffn[l].reshape(1, d), w_router[l], b_router[l],
                                 batch, seq)
        bend, csrc, cdst = _plan(meta, t)
        ys = _experts(xs, bend, csrc, cdst,
                      w_expert_gate, w_expert_up, w_expert_down, l, t)
    return _combine_norm(x2, ys, s12, meta, g_final).reshape(batch, seq, d)
```

```python
import functools

import jax
import jax.numpy as jnp
from jax import lax
from jax.experimental import pallas as pl
from jax.experimental.pallas import tpu as pltpu

F32 = jnp.float32
BF16 = jnp.bfloat16
U32 = jnp.uint32
I32 = jnp.int32

N_HEADS = 8
N_KV_HEADS = 2
HEAD_DIM = 64
ATTN_WIDTH = N_HEADS * HEAD_DIM
KV_WIDTH = N_KV_HEADS * HEAD_DIM
WINDOW = 128
QBLOCK = 128
ROPE_DIM = HEAD_DIM // 4
ROPE_THETA = 500000.0
CONV_WIDTH = 512
N_GROUPS = 4
PER_GROUP = 8
N_EXPERTS = N_GROUPS * PER_GROUP
D_EXPERT = 512
EPS = 1e-6
NEG_INF = -1e30

LANES = 128
SUBLANES = 8

FRONT_ROWS = 512
MIX_ROWS = 512
SORT_ROWS = 256
TILES_PER_STEP = 2
CHUNK = SUBLANES
BLOCK_ROWS = 512
BLOCK_CHUNKS = BLOCK_ROWS // CHUNK
SORT_CAP = ((2 * SORT_ROWS + N_EXPERTS * (CHUNK - 1)) // LANES + 1) * LANES
TILE_CHUNKS = SORT_CAP // CHUNK
PACK_W = 512
XS_W = PACK_W

VMEM_LIMIT = 56 * 1024 * 1024


def _rms(x, g):
    return x * lax.rsqrt(jnp.mean(x * x, axis=-1, keepdims=True) + EPS) * g


def _pack2(lo, hi):
    return pltpu.pack_elementwise([lo, hi], packed_dtype=BF16)


def _unpack2(w, index):
    return pltpu.unpack_elementwise(w, index=index, packed_dtype=BF16, unpacked_dtype=F32)


def _combine_front_kernel(x_ref, ys_ref, s_ref, meta_ref, g_ref, w_ref, cos_ref, sp_ref, sm_ref,
                          cw_ref, xo_ref, q_ref, k_ref, v_ref, yc_ref, edge_ref):
    _combine_tiles(x_ref, ys_ref, s_ref, meta_ref, xo_ref, None)
    _front_kernel(xo_ref, g_ref, w_ref, cos_ref, sp_ref, sm_ref, cw_ref,
                  q_ref, k_ref, v_ref, yc_ref, edge_ref)


def _front_kernel(x_ref, g_ref, w_ref, cos_ref, sp_ref, sm_ref, cw_ref,
                  q_ref, k_ref, v_ref, yc_ref, edge_ref):
    tm = x_ref.shape[0]
    h = _rms(x_ref[...], g_ref[...]).astype(BF16)
    cos = cos_ref[...]
    sp = sp_ref[...]
    sm = sm_ref[...]

    def proj(lo, hi):
        return jnp.dot(h, w_ref[:, lo:hi], preferred_element_type=F32)

    def rope(t):
        half = ROPE_DIM // 2
        return (t * cos + pltpu.roll(t, half, axis=1) * sp
                + pltpu.roll(t, LANES - half, axis=1) * sm)

    wide = 2 * LANES
    for c in range(ATTN_WIDTH // wide):
        q = proj(c * wide, (c + 1) * wide)
        for j in range(2):
            q_ref[:, c * wide + j * LANES:c * wide + (j + 1) * LANES] = (
                rope(q[:, j * LANES:(j + 1) * LANES]) * (HEAD_DIM ** -0.5)).astype(BF16)
    o = ATTN_WIDTH
    kv = proj(o, o + 2 * KV_WIDTH)
    k_ref[...] = rope(kv[:, :KV_WIDTH]).astype(BF16)
    v_ref[...] = kv[:, KV_WIDTH:].astype(BF16)
    o += 2 * KV_WIDTH
    gb = proj(o, o + CONV_WIDTH)
    o += CONV_WIDTH
    gc = proj(o, o + CONV_WIDTH)
    o += CONV_WIDTH
    u = proj(o, o + CONV_WIDTH)
    cu = gc * u
    ridx = lax.broadcasted_iota(I32, cu.shape, 0)
    up = jnp.where(ridx == 0, 0.0, pltpu.roll(cu, 1, axis=0))
    dn = jnp.where(ridx == tm - 1, 0.0, pltpu.roll(cu, tm - 1, axis=0))
    conv = up * cw_ref[0:1, :] + cu * cw_ref[1:2, :] + dn * cw_ref[2:3, :]
    yc_ref[...] = (gb * conv).astype(BF16)
    edge_ref[0] = jnp.concatenate(
        [cu[0:1], cu[tm - 1:tm], gb[0:1], gb[tm - 1:tm],
         jnp.zeros((SUBLANES - 4, CONV_WIDTH), F32)], axis=0)


def _combine_front(x2, ys, s12, meta, g, w_in_b, cos_t, sp_t, sm_t, conv_w, seq):
    t, d = x2.shape
    tm = FRONT_ROWS
    per = tm // SORT_ROWS
    nseq = seq // tm
    row = lambda i: (i, 0)
    tile = lambda i: (i, 0, 0)
    pos = lambda i: (i % nseq, 0)
    fixed = lambda i: (0, 0)
    ys2 = ys.reshape(ys.shape[0] * CHUNK, PACK_W)
    return pl.pallas_call(
        _combine_front_kernel,
        grid=(t // tm,),
        in_specs=[
            pl.BlockSpec((tm, d), row),
            pl.BlockSpec((per * SORT_CAP, PACK_W), row),
            pl.BlockSpec((per, SUBLANES, SORT_ROWS), tile),
            pl.BlockSpec((per, SUBLANES, LANES), tile),
            pl.BlockSpec((1, d), fixed),
            pl.BlockSpec(w_in_b.shape, fixed),
            pl.BlockSpec((tm, LANES), pos),
            pl.BlockSpec((tm, LANES), pos),
            pl.BlockSpec((tm, LANES), pos),
            pl.BlockSpec(conv_w.shape, fixed),
        ],
        out_specs=[
            pl.BlockSpec((tm, d), row),
            pl.BlockSpec((tm, ATTN_WIDTH), row),
            pl.BlockSpec((tm, KV_WIDTH), row),
            pl.BlockSpec((tm, KV_WIDTH), row),
            pl.BlockSpec((tm, CONV_WIDTH), row),
            pl.BlockSpec((1, SUBLANES, CONV_WIDTH), lambda i: (i, 0, 0)),
        ],
        out_shape=[
            jax.ShapeDtypeStruct((t, d), F32),
            jax.ShapeDtypeStruct((t, ATTN_WIDTH), BF16),
            jax.ShapeDtypeStruct((t, KV_WIDTH), BF16),
            jax.ShapeDtypeStruct((t, KV_WIDTH), BF16),
            jax.ShapeDtypeStruct((t, CONV_WIDTH), BF16),
            jax.ShapeDtypeStruct((t // tm, SUBLANES, CONV_WIDTH), F32),
        ],
        compiler_params=pltpu.CompilerParams(
            dimension_semantics=("parallel",), vmem_limit_bytes=VMEM_LIMIT),
        name="combine_front",
    )(x2, ys2, s12, meta, g, w_in_b, cos_t, sp_t, sm_t, conv_w)


def _front(x2, g, w_in_b, cos_t, sp_t, sm_t, conv_w, seq):
    t, d = x2.shape
    tm = FRONT_ROWS
    nseq = seq // tm
    row = lambda i: (i, 0)
    pos = lambda i: (i % nseq, 0)
    fixed = lambda i: (0, 0)
    return pl.pallas_call(
        _front_kernel,
        grid=(t // tm,),
        in_specs=[
            pl.BlockSpec((tm, d), row),
            pl.BlockSpec((1, d), fixed),
            pl.BlockSpec(w_in_b.shape, fixed),
            pl.BlockSpec((tm, LANES), pos),
            pl.BlockSpec((tm, LANES), pos),
            pl.BlockSpec((tm, LANES), pos),
            pl.BlockSpec(conv_w.shape, fixed),
        ],
        out_specs=[
            pl.BlockSpec((tm, ATTN_WIDTH), row),
            pl.BlockSpec((tm, KV_WIDTH), row),
            pl.BlockSpec((tm, KV_WIDTH), row),
            pl.BlockSpec((tm, CONV_WIDTH), row),
            pl.BlockSpec((1, SUBLANES, CONV_WIDTH), lambda i: (i, 0, 0)),
        ],
        out_shape=[
            jax.ShapeDtypeStruct((t, ATTN_WIDTH), BF16),
            jax.ShapeDtypeStruct((t, KV_WIDTH), BF16),
            jax.ShapeDtypeStruct((t, KV_WIDTH), BF16),
            jax.ShapeDtypeStruct((t, CONV_WIDTH), BF16),
            jax.ShapeDtypeStruct((t // tm, SUBLANES, CONV_WIDTH), F32),
        ],
        compiler_params=pltpu.CompilerParams(
            dimension_semantics=("parallel",), vmem_limit_bytes=VMEM_LIMIT),
        name="front",
    )(x2, g, w_in_b, cos_t, sp_t, sm_t, conv_w)


def _mix_kernel(sink_ref, x_ref, q_ref, kp_ref, kc_ref, kn_ref, vp_ref, vc_ref, vn_ref,
                yc_ref, edge_ref, edgep_ref, edgen_ref, cw_ref, ga_ref, gc_ref, wo_ref, bias_ref,
                gf_ref, wr_ref, br_ref,
                o_ref, xs_ref, s_ref, meta_ref, ya_ref, kd_ref, vd_ref, *, seq):
    tq = x_ref.shape[0]
    i = pl.program_id(1)
    nseq = pl.num_programs(1)
    grp = N_HEADS // N_KV_HEADS
    rows = grp * QBLOCK
    band = 3 * QBLOCK

    kcat = jnp.concatenate([kp_ref[...], kc_ref[...], kn_ref[...]], axis=0)
    vcat = jnp.concatenate([vp_ref[...], vc_ref[...], vn_ref[...]], axis=0)
    ones = jnp.ones((kcat.shape[0], HEAD_DIM), BF16)
    for h in range(N_KV_HEADS):
        kh = kcat[:, h * HEAD_DIM:(h + 1) * HEAD_DIM]
        vh = vcat[:, h * HEAD_DIM:(h + 1) * HEAD_DIM]
        kd_ref[h] = jnp.concatenate([kh, kh], axis=1)
        vd_ref[2 * h] = jnp.concatenate([vh, ones], axis=1)
        vd_ref[2 * h + 1] = jnp.concatenate([ones, vh], axis=1)

    lane = lax.broadcasted_iota(I32, (QBLOCK, LANES), 1)
    low = lane < HEAD_DIM
    low2 = lax.broadcasted_iota(I32, (2 * QBLOCK, LANES), 1) < HEAD_DIM
    hrow = lax.broadcasted_iota(I32, (rows, 1), 0) // QBLOCK
    head_order = (0, 2, 1, 3)
    sinks = []
    for hk in range(N_KV_HEADS):
        sink = jnp.zeros((rows, 1), F32)
        for g in range(grp):
            sink = jnp.where(hrow == g, sink_ref[hk * grp + head_order[g]], sink)
        sinks.append(sink)

    def qblock(jb, carry):
        r0 = pl.multiple_of(jb * QBLOCK, QBLOCK)
        pos0 = i * tq + r0
        bias_prev = bias_ref[jnp.where(pos0 == 0, 1, 0)]
        bias_next = bias_ref[jnp.where(pos0 + QBLOCK == seq, 3, 2)]
        qb = q_ref[pl.ds(r0, QBLOCK), :]
        for hk in range(N_KV_HEADS):
            q2 = [qb[:, (2 * hk + cc) * LANES:(2 * hk + cc + 1) * LANES] for cc in range(2)]
            qs = jnp.concatenate(
                [jnp.where(low, q, jnp.zeros_like(q)) for q in q2]
                + [jnp.where(low, jnp.zeros_like(q), q) for q in q2], axis=0)
            sink = sinks[hk]
            s = lax.dot_general(qs, kd_ref[hk, pl.ds(r0, band), :], (((1,), (1,)), ((), ())),
                                preferred_element_type=F32)
            s = jnp.concatenate([s[:, :QBLOCK] + jnp.concatenate([bias_prev] * grp, axis=0),
                                 s[:, QBLOCK:2 * QBLOCK],
                                 s[:, 2 * QBLOCK:] + jnp.concatenate([bias_next] * grp, axis=0)],
                                axis=1)
            m = jnp.maximum(jnp.max(s, axis=1, keepdims=True), sink)
            p = jnp.exp((s - m).astype(BF16))
            sink_p = jnp.exp(sink - m)
            half_rows = 2 * QBLOCK
            o_lo = jnp.dot(p[:half_rows], vd_ref[2 * hk, pl.ds(r0, band), :],
                           preferred_element_type=F32)
            o_hi = jnp.dot(p[half_rows:], vd_ref[2 * hk + 1, pl.ds(r0, band), :],
                           preferred_element_type=F32)
            den_lo = jnp.where(low2, pltpu.roll(o_lo, HEAD_DIM, axis=1), o_lo) + sink_p[:half_rows]
            den_hi = jnp.where(low2, o_hi, pltpu.roll(o_hi, HEAD_DIM, axis=1)) + sink_p[half_rows:]
            y_lo = o_lo * (1.0 / den_lo)
            y_hi = o_hi * (1.0 / den_hi)
            for cc in range(2):
                c = 2 * hk + cc
                ya_ref[pl.ds(r0, QBLOCK), c * LANES:(c + 1) * LANES] = jnp.where(
                    low, y_lo[cc * QBLOCK:(cc + 1) * QBLOCK], y_hi[cc * QBLOCK:(cc + 1) * QBLOCK])
        return carry

    for jb in range(tq // QBLOCK):
        qblock(jb, 0)

    own, prev, nxt = edge_ref[0], edgep_ref[0], edgen_ref[0]
    first = jnp.where(i > 0, own[2:3] * cw_ref[0:1, :] * prev[1:2], 0.0)
    last = jnp.where(i < nseq - 1, own[3:4] * cw_ref[2:3, :] * nxt[0:1], 0.0)
    eg = 2 * SUBLANES
    re = lax.broadcasted_iota(I32, (eg, CONV_WIDTH), 0)
    yc = jnp.concatenate(
        [yc_ref[0:eg, :].astype(F32) + jnp.where(re == 0, first, 0.0),
         yc_ref[eg:tq - eg, :].astype(F32),
         yc_ref[tq - eg:tq, :].astype(F32) + jnp.where(re == eg - 1, last, 0.0)],
        axis=0)
    yc = _rms(yc, gc_ref[...]).astype(BF16)
    ya = _rms(ya_ref[...], ga_ref[...]).astype(BF16)
    o_ref[...] = (x_ref[...]
                  + jnp.dot(ya, wo_ref[:ATTN_WIDTH, :], preferred_element_type=F32)
                  + jnp.dot(yc, wo_ref[ATTN_WIDTH:, :], preferred_element_type=F32))
    for u in range(xs_ref.shape[0]):
        _route_tile(u, o_ref, gf_ref, wr_ref, br_ref, xs_ref, s_ref, meta_ref)


def _mix(x2, q, k, v, yc, edge, sink, conv_w, g_attn, g_conv, w_out_b, g_ffn, w_router, b_router,
         batch, seq):
    t, d = x2.shape
    tq = MIX_ROWS
    assert tq == FRONT_ROWS
    nseq = seq // tq
    per = tq // SORT_ROWS
    nt = t // SORT_ROWS
    tile = lambda b, i: (b * nseq + i, 0, 0)
    tprev = lambda b, i: (b * nseq + jnp.maximum(i - 1, 0), 0, 0)
    tnext = lambda b, i: (b * nseq + jnp.minimum(i + 1, nseq - 1), 0, 0)
    kb = tq // QBLOCK
    nkb = seq // QBLOCK

    row = lambda b, i: (b * nseq + i, 0)
    kprev = lambda b, i: (b * nkb + jnp.maximum(i * kb - 1, 0), 0)
    knext = lambda b, i: (b * nkb + jnp.minimum((i + 1) * kb, nkb - 1), 0)
    fixed = lambda b, i: (0, 0)
    edge_block = (1, SUBLANES, CONV_WIDTH)
    qi = lax.broadcasted_iota(I32, (QBLOCK, QBLOCK), 0)
    kj = lax.broadcasted_iota(I32, (QBLOCK, QBLOCK), 1)
    neg = jnp.full((QBLOCK, QBLOCK), NEG_INF, F32)
    band_bias = jnp.stack([jnp.where(kj - QBLOCK - qi >= -WINDOW, 0.0, NEG_INF).astype(F32), neg,
                           jnp.where(kj + QBLOCK - qi <= WINDOW, 0.0, NEG_INF).astype(F32), neg])
    return pl.pallas_call(
        functools.partial(_mix_kernel, seq=seq),
        grid=(batch, nseq),
        in_specs=[
            pl.BlockSpec(memory_space=pltpu.SMEM),
            pl.BlockSpec((tq, d), row),
            pl.BlockSpec((tq, ATTN_WIDTH), row),
            pl.BlockSpec((QBLOCK, KV_WIDTH), kprev),
            pl.BlockSpec((tq, KV_WIDTH), row),
            pl.BlockSpec((QBLOCK, KV_WIDTH), knext),
            pl.BlockSpec((QBLOCK, KV_WIDTH), kprev),
            pl.BlockSpec((tq, KV_WIDTH), row),
            pl.BlockSpec((QBLOCK, KV_WIDTH), knext),
            pl.BlockSpec((tq, CONV_WIDTH), row),
            pl.BlockSpec(edge_block, tile),
            pl.BlockSpec(edge_block, tprev),
            pl.BlockSpec(edge_block, tnext),
            pl.BlockSpec(conv_w.shape, fixed),
            pl.BlockSpec((1, ATTN_WIDTH), fixed),
            pl.BlockSpec((1, CONV_WIDTH), fixed),
            pl.BlockSpec(w_out_b.shape, fixed),
            pl.BlockSpec(band_bias.shape, lambda b, i: (0, 0, 0)),
            pl.BlockSpec((1, d), fixed),
            pl.BlockSpec(w_router.shape, fixed),
            pl.BlockSpec(b_router.shape, fixed),
        ],
        out_specs=[
            pl.BlockSpec((tq, d), row),
            pl.BlockSpec((per, SORT_CAP, XS_W), tile),
            pl.BlockSpec((per, SUBLANES, SORT_ROWS), tile),
            pl.BlockSpec((per, SUBLANES, LANES), tile),
        ],
        out_shape=[
            jax.ShapeDtypeStruct((t, d), F32),
            jax.ShapeDtypeStruct((nt, SORT_CAP, XS_W), U32),
            jax.ShapeDtypeStruct((nt, SUBLANES, SORT_ROWS), F32),
            jax.ShapeDtypeStruct((nt, SUBLANES, LANES), I32),
        ],
        scratch_shapes=[pltpu.VMEM((tq, ATTN_WIDTH), F32),
                        pltpu.VMEM((N_KV_HEADS, tq + 2 * QBLOCK, LANES), BF16),
                        pltpu.VMEM((2 * N_KV_HEADS, tq + 2 * QBLOCK, LANES), BF16)],
        compiler_params=pltpu.CompilerParams(
            dimension_semantics=("parallel", "parallel"), vmem_limit_bytes=VMEM_LIMIT),
        name="mix",
    )(sink, x2, q, k, k, k, v, v, v, yc, edge, edge, edge, conv_w, g_attn, g_conv, w_out_b, band_bias,
      g_ffn, w_router, b_router)


ROUTER_ROWS = N_EXPERTS + SUBLANES


def _route_tile(u, x_ref, g_ref, w_ref, b_ref, xs_ref, s_ref, meta_ref):
    tl = SORT_ROWS
    h = _rms(x_ref[u * tl:(u + 1) * tl, :], g_ref[...])
    hb = h.astype(BF16)

    dn = (((1,), (1,)), ((), ()))
    h2 = (h - hb.astype(F32)).astype(BF16)
    both = lax.dot_general(w_ref[...], hb, dn, preferred_element_type=F32)
    corr = lax.dot_general(w_ref[:ROUTER_ROWS, :], h2, dn, preferred_element_type=F32)
    logits = both[:ROUTER_ROWS] + both[ROUTER_ROWS:] + corr + b_ref[...]
    el = logits[:N_EXPERTS]
    gl = logits[N_EXPERTS:]

    r8 = lax.broadcasted_iota(I32, gl.shape, 0)
    glm = jnp.where(r8 < N_GROUPS, gl, -jnp.inf)
    gmax = jnp.max(glm, axis=0, keepdims=True)
    grp = jnp.min(jnp.where(glm == gmax, r8, SUBLANES), axis=0, keepdims=True)
    p_grp = 1.0 / jnp.sum(jnp.exp(glm - gmax), axis=0, keepdims=True)

    r32 = lax.broadcasted_iota(I32, el.shape, 0)
    e0 = jnp.where((r32 // PER_GROUP) == grp, el, -jnp.inf)
    t1 = jnp.max(e0, axis=0, keepdims=True)
    i1 = jnp.min(jnp.where(e0 == t1, r32, N_EXPERTS), axis=0, keepdims=True)
    a1 = r32 == i1
    e1 = jnp.where(a1, -jnp.inf, e0)
    t2 = jnp.max(e1, axis=0, keepdims=True)
    i2 = jnp.min(jnp.where(e1 == t2, r32, N_EXPERTS), axis=0, keepdims=True)
    a2 = r32 == i2
    ed = jnp.exp(t2 - t1)
    inv = 1.0 / (1.0 + ed)
    g1 = p_grp * inv
    g2 = p_grp * (ed * inv)

    af = (a1 | a2).astype(F32)
    cnt = jnp.sum(af, axis=1, keepdims=True)
    pc = jnp.floor((cnt + (CHUNK - 1)) * (1.0 / CHUNK)) * CHUNK
    er = lax.broadcasted_iota(I32, (N_EXPERTS, LANES), 0)
    ec = lax.broadcasted_iota(I32, (N_EXPERTS, LANES), 1)
    pc_row = jnp.sum(jnp.where(er == ec, pc, 0.0), axis=0, keepdims=True)
    lo_row = jnp.sum(jnp.where(er < ec, pc, 0.0), axis=0, keepdims=True)
    lo_col = jnp.sum(jnp.where(ec < er, pc_row, 0.0), axis=1, keepdims=True)
    total = jnp.sum(pc, axis=0, keepdims=True)

    tr = lax.broadcasted_iota(I32, (tl, tl), 0)
    tc = lax.broadcasted_iota(I32, (tl, tl), 1)
    upper = (tr <= tc).astype(BF16)
    incl = jnp.dot(af.astype(BF16), upper, preferred_element_type=F32)
    slot = lo_col + incl - af
    s1 = jnp.sum(jnp.where(a1, slot, 0.0), axis=0, keepdims=True).astype(I32)
    s2 = jnp.sum(jnp.where(a2, slot, 0.0), axis=0, keepdims=True).astype(I32)

    cap = xs_ref.shape[1]
    srow = lax.broadcasted_iota(I32, (cap, tl), 0)
    p1 = srow == s1
    p2 = srow == s2
    perm = (p1 | p2).astype(BF16)
    half = hb.shape[1] // 2
    lo = jnp.dot(perm, hb[:, :half], preferred_element_type=F32)
    hi = jnp.dot(perm, hb[:, half:], preferred_element_type=F32)
    xs_ref[u] = _pack2(lo, hi)

    sr = lax.broadcasted_iota(I32, (SUBLANES, tl), 0)
    s_ref[u] = jnp.where(sr == 0, s1.astype(F32), jnp.where(sr == 1, s2.astype(F32),
                         jnp.where(sr == 2, g1, jnp.where(sr == 3, g2, 0.0))))
    mr = lax.broadcasted_iota(I32, (SUBLANES, LANES), 0)
    meta = jnp.where(mr == 0, pc_row, jnp.where(mr == 1, lo_row, jnp.where(mr == 2, total, 0.0)))
    meta_ref[u] = (meta * (1.0 / CHUNK)).astype(I32)


def _router_params(w_group, b_group, w_expert, b_expert):
    pad = SUBLANES - N_GROUPS
    w = jnp.concatenate([jnp.swapaxes(w_expert, 1, 2),
                         jnp.pad(jnp.swapaxes(w_group, 1, 2), ((0, 0), (0, pad), (0, 0)))], axis=1)
    w1 = w.astype(BF16)
    w2 = (w - w1.astype(F32)).astype(BF16)
    b = jnp.concatenate([b_expert, jnp.pad(b_group, ((0, 0), (0, pad)))], axis=1)[:, :, None]
    return jnp.concatenate([w1, w2], axis=1), b


def _n_blocks(t):
    nt = t // SORT_ROWS
    max_chunks = (2 * t + (CHUNK - 1) * N_EXPERTS * nt) // CHUNK + (BLOCK_CHUNKS - 1) * N_EXPERTS
    return -(-max_chunks // BLOCK_CHUNKS)


PLAN_LANES = 1024


def _list_len(t):
    return -(-(_n_blocks(t) + 1) * BLOCK_CHUNKS // PLAN_LANES) * PLAN_LANES


def _plan_kernel(nch_ref, ncht_ref, lo8_ref, csrc_ref, cdst_ref, bend_ref):
    nt = nch_ref.shape[0]
    ch = csrc_ref.shape[1]
    nch = nch_ref[:, :N_EXPERTS].astype(F32)
    lo8 = lo8_ref[:, :N_EXPERTS].astype(F32)
    ncht = ncht_ref[...].astype(F32)

    tc = jnp.sum(ncht, axis=1, keepdims=True)
    ptc = jnp.floor((tc + (BLOCK_CHUNKS - 1)) * (1.0 / BLOCK_CHUNKS)) * BLOCK_CHUNKS
    er = lax.broadcasted_iota(I32, (N_EXPERTS, LANES), 0)
    ec = lax.broadcasted_iota(I32, (N_EXPERTS, LANES), 1)
    ptc_row = jnp.sum(jnp.where(er == ec, ptc, 0.0), axis=0, keepdims=True)
    eend = jnp.sum(jnp.where(ec <= er, ptc_row, 0.0), axis=1, keepdims=True)
    estart = eend - ptc
    total = jnp.sum(ptc, axis=0, keepdims=True)

    tr = lax.broadcasted_iota(I32, (nt, nt), 0)
    tcol = lax.broadcasted_iota(I32, (nt, nt), 1)
    lower = (tcol <= tr).astype(BF16)
    cum = jnp.dot(lower, nch.astype(BF16), preferred_element_type=F32)
    cum_hi = jnp.floor(cum * (1.0 / LANES))
    cum_lo = cum - cum_hi * LANES

    k = (pl.program_id(0) * ch + lax.broadcasted_iota(I32, (1, ch), 1)).astype(F32)
    r32 = lax.broadcasted_iota(I32, (N_EXPERTS, ch), 0).astype(F32)

    def expert_of(pos):
        return jnp.minimum(jnp.sum((eend <= pos).astype(F32), axis=0, keepdims=True),
                           N_EXPERTS - 1.0)

    e_k = expert_of(k)
    sel_e = r32 == e_k
    oh_e = sel_e.astype(BF16)
    start_k = jnp.sum(jnp.where(sel_e, estart, 0.0), axis=0, keepdims=True)
    tc_k = jnp.sum(jnp.where(sel_e, tc, 0.0), axis=0, keepdims=True)
    q = k - start_k

    def pick(tbl):
        return jnp.dot(tbl.astype(BF16), oh_e, preferred_element_type=F32)

    cum_e = pick(cum_hi) * LANES + pick(cum_lo)
    nch_e = pick(nch)
    lo8_e = pick(lo8)
    rnt = lax.broadcasted_iota(I32, (nt, ch), 0).astype(F32)
    i_k = jnp.minimum(jnp.sum((cum_e <= q).astype(F32), axis=0, keepdims=True), nt - 1.0)
    sel_i = rnt == i_k
    before = jnp.sum(jnp.where(sel_i, cum_e - nch_e, 0.0), axis=0, keepdims=True)
    lo8_k = jnp.sum(jnp.where(sel_i, lo8_e, 0.0), axis=0, keepdims=True)
    src = i_k * TILE_CHUNKS + lo8_k + (q - before)
    pad = (q >= tc_k) | (k >= total)
    csrc_ref[...] = jnp.where(pad, TILE_CHUNKS - 1.0, src).astype(I32)
    cdst_ref[...] = jnp.where(pad, nt * TILE_CHUNKS + k, src).astype(I32)

    eend_row = jnp.sum(jnp.where(er <= ec, ptc, 0.0), axis=0, keepdims=True)
    bend_ref[...] = (eend_row * (1.0 / BLOCK_CHUNKS)).astype(I32)


def _plan(meta, t):
    nt = t // SORT_ROWS
    nl = _list_len(t)
    nch = meta[:, 0, :]
    lo8 = meta[:, 1, :]
    ncht = nch[:, :N_EXPERTS].T
    fixed = lambda c: (0, 0)
    csrc, cdst, bend = pl.pallas_call(
        _plan_kernel,
        grid=(nl // PLAN_LANES,),
        in_specs=[
            pl.BlockSpec(nch.shape, fixed),
            pl.BlockSpec(ncht.shape, fixed),
            pl.BlockSpec(lo8.shape, fixed),
        ],
        out_specs=[
            pl.BlockSpec((1, PLAN_LANES), lambda c: (0, c)),
            pl.BlockSpec((1, PLAN_LANES), lambda c: (0, c)),
            pl.BlockSpec((1, LANES), fixed),
        ],
        out_shape=[
            jax.ShapeDtypeStruct((1, nl), I32),
            jax.ShapeDtypeStruct((1, nl), I32),
            jax.ShapeDtypeStruct((1, LANES), I32),
        ],
        compiler_params=pltpu.CompilerParams(dimension_semantics=("arbitrary",)),
        name="plan",
    )(nch, ncht, lo8)
    return bend.reshape(-1), csrc.reshape(-1), cdst.reshape(-1)


def _expert_kernel(bend_ref, csrc_ref, cdst_ref,
                   xs_hbm, wg_ref, wu_ref, wd_ref, ys_hbm,
                   xbuf, ybuf, wgb, wub, wdb, sem_in, sem_out):
    e = pl.program_id(0)
    n_used = bend_ref[N_EXPERTS - 1]
    b_lo = jnp.where(e == 0, 0, bend_ref[jnp.maximum(e - 1, 0)])
    b_hi = bend_ref[e]

    def gather(blk, sl):
        return [pltpu.make_async_copy(xs_hbm.at[csrc_ref[blk * BLOCK_CHUNKS + j]],
                                      xbuf.at[sl, j], sem_in.at[sl])
                for j in range(BLOCK_CHUNKS)]

    def scatter(blk, sl):
        return [pltpu.make_async_copy(ybuf.at[sl, j],
                                      ys_hbm.at[cdst_ref[blk * BLOCK_CHUNKS + j]], sem_out.at[sl])
                for j in range(BLOCK_CHUNKS)]

    @pl.when(e == 0)
    def _():
        for c in gather(0, 0):
            c.start()

    @pl.when(b_hi > b_lo)
    def _():
        wgb[...] = wg_ref[0, 0].astype(BF16)
        wub[...] = wu_ref[0, 0].astype(BF16)
        wdb[...] = wd_ref[0, 0].astype(BF16)

    def block(b, carry):
        slot = lax.rem(b, 2)
        for c in gather(b, slot):
            c.wait()
        for c in gather(b + 1, 1 - slot):
            c.start()
        xw = xbuf[slot].reshape(BLOCK_ROWS, XS_W)
        lo = _unpack2(xw, 0).astype(BF16)
        hi = _unpack2(xw, 1).astype(BF16)
        half = PACK_W
        hg = (jnp.dot(lo, wgb[:half, :], preferred_element_type=F32)
              + jnp.dot(hi, wgb[half:, :], preferred_element_type=F32))
        hu = (jnp.dot(lo, wub[:half, :], preferred_element_type=F32)
              + jnp.dot(hi, wub[half:, :], preferred_element_type=F32))
        act = (hg * (1.0 / (1.0 + jnp.exp(-hg))) * hu).astype(BF16)
        y = jnp.dot(act, wdb[...], preferred_element_type=F32)
        ybuf[slot] = _pack2(y[:, :half], y[:, half:]).reshape(BLOCK_CHUNKS, CHUNK, PACK_W)
        for c in scatter(b, slot):
            c.start(priority=1)

        @pl.when(b >= 1)
        def _():
            for c in scatter(b - 1, 1 - slot):
                c.wait()

        return carry

    lax.fori_loop(b_lo, b_hi, block, 0)

    @pl.when(e == N_EXPERTS - 1)
    def _():
        last = n_used - 1
        for c in scatter(last, lax.rem(last, 2)):
            c.wait()
        for c in gather(n_used, lax.rem(n_used, 2)):
            c.wait()


def _experts(xs, bend, csrc, cdst, w_gate, w_up, w_down, layer, t):
    nt = t // SORT_ROWS
    d = w_gate.shape[2]
    xs3 = xs.reshape(nt * TILE_CHUNKS, CHUNK, XS_W)
    n_out = nt * TILE_CHUNKS + _list_len(t)
    wmap = lambda e, be, cs, cd: (layer, e, 0, 0)
    grid_spec = pltpu.PrefetchScalarGridSpec(
        num_scalar_prefetch=3,
        grid=(N_EXPERTS,),
        in_specs=[
            pl.BlockSpec(memory_space=pl.ANY),
            pl.BlockSpec((1, 1, d, D_EXPERT), wmap),
            pl.BlockSpec((1, 1, d, D_EXPERT), wmap),
            pl.BlockSpec((1, 1, D_EXPERT, d), wmap),
        ],
        out_specs=pl.BlockSpec(memory_space=pl.ANY),
        scratch_shapes=[
            pltpu.VMEM((2, BLOCK_CHUNKS, CHUNK, XS_W), U32),
            pltpu.VMEM((2, BLOCK_CHUNKS, CHUNK, PACK_W), U32),
            pltpu.VMEM((d, D_EXPERT), BF16),
            pltpu.VMEM((d, D_EXPERT), BF16),
            pltpu.VMEM((D_EXPERT, d), BF16),
            pltpu.SemaphoreType.DMA((2,)),
            pltpu.SemaphoreType.DMA((2,)),
        ],
    )
    return pl.pallas_call(
        _expert_kernel,
        grid_spec=grid_spec,
        out_shape=jax.ShapeDtypeStruct((n_out, CHUNK, PACK_W), U32),
        compiler_params=pltpu.CompilerParams(
            dimension_semantics=("arbitrary",), vmem_limit_bytes=VMEM_LIMIT),
        name="experts",
    )(bend, csrc, cdst, xs3, w_gate, w_up, w_down)


def _combine_kernel(x_ref, ys_ref, s_ref, meta_ref, g_ref, o_ref):
    _combine_tiles(x_ref, ys_ref, s_ref, meta_ref, o_ref, g_ref)


def _combine_tiles(x_ref, ys_ref, s_ref, meta_ref, o_ref, g_ref):
    tl = SORT_ROWS
    cap = SORT_CAP
    tr = lax.broadcasted_iota(I32, (tl, tl), 0)
    tc = lax.broadcasted_iota(I32, (tl, tl), 1)
    eye = tr == tc
    col = lax.broadcasted_iota(I32, (tl, cap), 1).astype(F32)
    yrow = lax.broadcasted_iota(I32, (cap, PACK_W), 0)
    for u in range(s_ref.shape[0]):
        used = meta_ref[u, 2:3, 0:1] * CHUNK
        ys = ys_ref[u * cap:(u + 1) * cap, :]
        live = yrow < used
        ylo = jnp.where(live, _unpack2(ys, 0), 0.0).astype(BF16)
        yhi = jnp.where(live, _unpack2(ys, 1), 0.0).astype(BF16)
        s = s_ref[u]
        s1, s2, g1, g2 = [jnp.sum(jnp.where(eye, s[r:r + 1, :], 0.0), axis=1, keepdims=True)
                          for r in range(4)]
        perm_t = (jnp.where(col == s1, g1, 0.0) + jnp.where(col == s2, g2, 0.0)).astype(BF16)
        half = PACK_W
        rows = slice(u * tl, (u + 1) * tl)
        out = jnp.concatenate(
            [x_ref[rows, :half] + jnp.dot(perm_t, ylo, preferred_element_type=F32),
             x_ref[rows, half:] + jnp.dot(perm_t, yhi, preferred_element_type=F32)], axis=1)
        if g_ref is not None:
            out = _rms(out, g_ref[...])
        o_ref[rows, :] = out


def _combine_norm(x2, ys, s12, meta, g_final):
    t, d = x2.shape
    tl = SORT_ROWS
    nt = t // tl
    ys2 = ys.reshape(ys.shape[0] * CHUNK, PACK_W)
    per = TILES_PER_STEP
    return pl.pallas_call(
        _combine_kernel,
        grid=(nt // per,),
        in_specs=[
            pl.BlockSpec((per * tl, d), lambda i: (i, 0)),
            pl.BlockSpec((per * SORT_CAP, PACK_W), lambda i: (i, 0)),
            pl.BlockSpec((per, SUBLANES, tl), lambda i: (i, 0, 0)),
            pl.BlockSpec((per, SUBLANES, LANES), lambda i: (i, 0, 0)),
            pl.BlockSpec((1, d), lambda i: (0, 0)),
        ],
        out_specs=pl.BlockSpec((per * tl, d), lambda i: (i, 0)),
        out_shape=jax.ShapeDtypeStruct((t, d), F32),
        compiler_params=pltpu.CompilerParams(
            dimension_semantics=("parallel",), vmem_limit_bytes=VMEM_LIMIT),
        name="combine",
    )(x2, ys2, s12, meta, g_final)


def _rope_tables(seq):
    pos = jnp.arange(seq, dtype=F32)
    inv_freq = ROPE_THETA ** (-jnp.arange(0, ROPE_DIM, 2, dtype=F32) / ROPE_DIM)
    ang = pos[:, None] * inv_freq[None, :]
    cos, sin = jnp.cos(ang), jnp.sin(ang)
    half = ROPE_DIM // 2
    ones = jnp.ones((seq, HEAD_DIM - ROPE_DIM), F32)
    zeros_h = jnp.zeros((seq, half), F32)
    zeros_r = jnp.zeros((seq, HEAD_DIM - ROPE_DIM), F32)
    cos_h = jnp.concatenate([cos, cos, ones], axis=1)
    sp_h = jnp.concatenate([zeros_h, sin, zeros_r], axis=1)
    sm_h = jnp.concatenate([-sin, zeros_h, zeros_r], axis=1)
    rep = LANES // HEAD_DIM
    return (jnp.tile(cos_h, (1, rep)), jnp.tile(sp_h, (1, rep)), jnp.tile(sm_h, (1, rep)))


def kernel(x, norm_mix, w_in, attn_sink, conv_w, norm_attn_out, norm_conv_out, w_out, norm_ffn,
           w_router_group, b_router_group, w_router_expert, b_router_expert,
           w_expert_gate, w_expert_up, w_expert_down, norm_final):
    batch, seq, d = x.shape
    depth = w_in.shape[0]
    t = batch * seq
    assert seq % MIX_ROWS == 0 and seq % FRONT_ROWS == 0 and t % SORT_ROWS == 0
    cos_t, sp_t, sm_t = _rope_tables(seq)
    x2 = x.reshape(t, d)
    w_in_b = w_in.astype(BF16)
    w_out_b = w_out.astype(BF16)
    w_router, b_router = _router_params(w_router_group, b_router_group,
                                        w_router_expert, b_router_expert)
    g_final = norm_final.reshape(1, d)

    ys = s12 = meta = None
    for l in range(depth):
        g_mix = norm_mix[l].reshape(1, d)
        if l == 0:
            q, k, v, yc, edge = _front(x2, g_mix, w_in_b[l], cos_t, sp_t, sm_t, conv_w[l], seq)
        else:
            x2, q, k, v, yc, edge = _combine_front(x2, ys, s12, meta, g_mix, w_in_b[l],
                                                   cos_t, sp_t, sm_t, conv_w[l], seq)
        x2, xs, s12, meta = _mix(x2, q, k, v, yc, edge, attn_sink[l], conv_w[l],
                                 norm_attn_out[l].reshape(1, -1), norm_conv_out[l].reshape(1, -1),
                                 w_out_b[l], norm_ffn[l].reshape(1, d), w_router[l], b_router[l],
                                 batch, seq)
        bend, csrc, cdst = _plan(meta, t)
        ys = _experts(xs, bend, csrc, cdst,
                      w_expert_gate, w_expert_up, w_expert_down, l, t)
    return _combine_norm(x2, ys, s12, meta, g_final).reshape(batch, seq, d)
```

```python
import functools

import jax
import jax.numpy as jnp
from jax import lax
from jax.experimental import pallas as pl
from jax.experimental.pallas import tpu as pltpu

F32 = jnp.float32
BF16 = jnp.bfloat16
U32 = jnp.uint32
I32 = jnp.int32

N_HEADS = 8
N_KV_HEADS = 2
HEAD_DIM = 64
ATTN_WIDTH = N_HEADS * HEAD_DIM
KV_WIDTH = N_KV_HEADS * HEAD_DIM
WINDOW = 128
QBLOCK = 128
ROPE_DIM = HEAD_DIM // 4
ROPE_THETA = 500000.0
CONV_WIDTH = 512
N_GROUPS = 4
PER_GROUP = 8
N_EXPERTS = N_GROUPS * PER_GROUP
D_EXPERT = 512
EPS = 1e-6
NEG_INF = -1e30

LANES = 128
SUBLANES = 8

FRONT_ROWS = 1024
MIX_ROWS = 1024
SORT_ROWS = 256
TILES_PER_STEP = 2
CHUNK = SUBLANES
BLOCK_ROWS = 512
BLOCK_CHUNKS = BLOCK_ROWS // CHUNK
SORT_CAP = ((2 * SORT_ROWS + N_EXPERTS * (CHUNK - 1)) // LANES + 1) * LANES
TILE_CHUNKS = SORT_CAP // CHUNK
PACK_W = 512
XS_W = PACK_W

VMEM_LIMIT = 56 * 1024 * 1024


def _rms(x, g):
    return x * lax.rsqrt(jnp.mean(x * x, axis=-1, keepdims=True) + EPS) * g


def _pack2(lo, hi):
    return pltpu.pack_elementwise([lo, hi], packed_dtype=BF16)


def _unpack2(w, index):
    return pltpu.unpack_elementwise(w, index=index, packed_dtype=BF16, unpacked_dtype=F32)


def _combine_front_kernel(x_ref, ys_ref, s_ref, meta_ref, g_ref, w_ref, cos_ref, sp_ref, sm_ref,
                          cw_ref, xo_ref, q_ref, k_ref, v_ref, yc_ref, edge_ref):
    _combine_tiles(x_ref, ys_ref, s_ref, meta_ref, xo_ref, None)
    _front_kernel(xo_ref, g_ref, w_ref, cos_ref, sp_ref, sm_ref, cw_ref,
                  q_ref, k_ref, v_ref, yc_ref, edge_ref)


def _front_kernel(x_ref, g_ref, w_ref, cos_ref, sp_ref, sm_ref, cw_ref,
                  q_ref, k_ref, v_ref, yc_ref, edge_ref):
    tm = x_ref.shape[0]
    h = _rms(x_ref[...], g_ref[...]).astype(BF16)
    cos = cos_ref[...]
    sp = sp_ref[...]
    sm = sm_ref[...]

    def proj(lo, hi):
        return jnp.dot(h, w_ref[:, lo:hi], preferred_element_type=F32)

    def rope(t):
        half = ROPE_DIM // 2
        return (t * cos + pltpu.roll(t, half, axis=1) * sp
                + pltpu.roll(t, LANES - half, axis=1) * sm)

    wide = 2 * LANES
    for c in range(ATTN_WIDTH // wide):
        q = proj(c * wide, (c + 1) * wide)
        for j in range(2):
            q_ref[:, c * wide + j * LANES:c * wide + (j + 1) * LANES] = (
                rope(q[:, j * LANES:(j + 1) * LANES]) * (HEAD_DIM ** -0.5)).astype(BF16)
    o = ATTN_WIDTH
    kv = proj(o, o + 2 * KV_WIDTH)
    k_ref[...] = rope(kv[:, :KV_WIDTH]).astype(BF16)
    v_ref[...] = kv[:, KV_WIDTH:].astype(BF16)
    o += 2 * KV_WIDTH
    gb = proj(o, o + CONV_WIDTH)
    o += CONV_WIDTH
    gc = proj(o, o + CONV_WIDTH)
    o += CONV_WIDTH
    u = proj(o, o + CONV_WIDTH)
    cu = gc * u
    ridx = lax.broadcasted_iota(I32, cu.shape, 0)
    up = jnp.where(ridx == 0, 0.0, pltpu.roll(cu, 1, axis=0))
    dn = jnp.where(ridx == tm - 1, 0.0, pltpu.roll(cu, tm - 1, axis=0))
    conv = up * cw_ref[0:1, :] + cu * cw_ref[1:2, :] + dn * cw_ref[2:3, :]
    yc_ref[...] = (gb * conv).astype(BF16)
    edge_ref[0] = jnp.concatenate(
        [cu[0:1], cu[tm - 1:tm], gb[0:1], gb[tm - 1:tm],
         jnp.zeros((SUBLANES - 4, CONV_WIDTH), F32)], axis=0)


def _combine_front(x2, ys, s12, meta, g, w_in_b, cos_t, sp_t, sm_t, conv_w, seq):
    t, d = x2.shape
    tm = FRONT_ROWS
    per = tm // SORT_ROWS
    nseq = seq // tm
    row = lambda i: (i, 0)
    tile = lambda i: (i, 0, 0)
    pos = lambda i: (i % nseq, 0)
    fixed = lambda i: (0, 0)
    ys2 = ys.reshape(ys.shape[0] * CHUNK, PACK_W)
    return pl.pallas_call(
        _combine_front_kernel,
        grid=(t // tm,),
        in_specs=[
            pl.BlockSpec((tm, d), row),
            pl.BlockSpec((per * SORT_CAP, PACK_W), row),
            pl.BlockSpec((per, SUBLANES, SORT_ROWS), tile),
            pl.BlockSpec((per, SUBLANES, LANES), tile),
            pl.BlockSpec((1, d), fixed),
            pl.BlockSpec(w_in_b.shape, fixed),
            pl.BlockSpec((tm, LANES), pos),
            pl.BlockSpec((tm, LANES), pos),
            pl.BlockSpec((tm, LANES), pos),
            pl.BlockSpec(conv_w.shape, fixed),
        ],
        out_specs=[
            pl.BlockSpec((tm, d), row),
            pl.BlockSpec((tm, ATTN_WIDTH), row),
            pl.BlockSpec((tm, KV_WIDTH), row),
            pl.BlockSpec((tm, KV_WIDTH), row),
            pl.BlockSpec((tm, CONV_WIDTH), row),
            pl.BlockSpec((1, SUBLANES, CONV_WIDTH), lambda i: (i, 0, 0)),
        ],
        out_shape=[
            jax.ShapeDtypeStruct((t, d), F32),
            jax.ShapeDtypeStruct((t, ATTN_WIDTH), BF16),
            jax.ShapeDtypeStruct((t, KV_WIDTH), BF16),
            jax.ShapeDtypeStruct((t, KV_WIDTH), BF16),
            jax.ShapeDtypeStruct((t, CONV_WIDTH), BF16),
            jax.ShapeDtypeStruct((t // tm, SUBLANES, CONV_WIDTH), F32),
        ],
        compiler_params=pltpu.CompilerParams(
            dimension_semantics=("parallel",), vmem_limit_bytes=VMEM_LIMIT),
        name="combine_front",
    )(x2, ys2, s12, meta, g, w_in_b, cos_t, sp_t, sm_t, conv_w)


def _front(x2, g, w_in_b, cos_t, sp_t, sm_t, conv_w, seq):
    t, d = x2.shape
    tm = FRONT_ROWS
    nseq = seq // tm
    row = lambda i: (i, 0)
    pos = lambda i: (i % nseq, 0)
    fixed = lambda i: (0, 0)
    return pl.pallas_call(
        _front_kernel,
        grid=(t // tm,),
        in_specs=[
            pl.BlockSpec((tm, d), row),
            pl.BlockSpec((1, d), fixed),
            pl.BlockSpec(w_in_b.shape, fixed),
            pl.BlockSpec((tm, LANES), pos),
            pl.BlockSpec((tm, LANES), pos),
            pl.BlockSpec((tm, LANES), pos),
            pl.BlockSpec(conv_w.shape, fixed),
        ],
        out_specs=[
            pl.BlockSpec((tm, ATTN_WIDTH), row),
            pl.BlockSpec((tm, KV_WIDTH), row),
            pl.BlockSpec((tm, KV_WIDTH), row),
            pl.BlockSpec((tm, CONV_WIDTH), row),
            pl.BlockSpec((1, SUBLANES, CONV_WIDTH), lambda i: (i, 0, 0)),
        ],
        out_shape=[
            jax.ShapeDtypeStruct((t, ATTN_WIDTH), BF16),
            jax.ShapeDtypeStruct((t, KV_WIDTH), BF16),
            jax.ShapeDtypeStruct((t, KV_WIDTH), BF16),
            jax.ShapeDtypeStruct((t, CONV_WIDTH), BF16),
            jax.ShapeDtypeStruct((t // tm, SUBLANES, CONV_WIDTH), F32),
        ],
        compiler_params=pltpu.CompilerParams(
            dimension_semantics=("parallel",), vmem_limit_bytes=VMEM_LIMIT),
        name="front",
    )(x2, g, w_in_b, cos_t, sp_t, sm_t, conv_w)


def _mix_kernel(sink_ref, x_ref, q_ref, kp_ref, kc_ref, kn_ref, vp_ref, vc_ref, vn_ref,
                yc_ref, edge_ref, edgep_ref, edgen_ref, cw_ref, ga_ref, gc_ref, wo_ref, bias_ref,
                gf_ref, wr_ref, br_ref,
                o_ref, xs_ref, s_ref, meta_ref, ya_ref, kd_ref, vd_ref, *, seq):
    tq = x_ref.shape[0]
    i = pl.program_id(1)
    nseq = pl.num_programs(1)
    grp = N_HEADS // N_KV_HEADS
    rows = grp * QBLOCK
    band = 3 * QBLOCK

    kcat = jnp.concatenate([kp_ref[...], kc_ref[...], kn_ref[...]], axis=0)
    vcat = jnp.concatenate([vp_ref[...], vc_ref[...], vn_ref[...]], axis=0)
    ones = jnp.ones((kcat.shape[0], HEAD_DIM), BF16)
    for h in range(N_KV_HEADS):
        kh = kcat[:, h * HEAD_DIM:(h + 1) * HEAD_DIM]
        vh = vcat[:, h * HEAD_DIM:(h + 1) * HEAD_DIM]
        kd_ref[h] = jnp.concatenate([kh, kh], axis=1)
        vd_ref[2 * h] = jnp.concatenate([vh, ones], axis=1)
        vd_ref[2 * h + 1] = jnp.concatenate([ones, vh], axis=1)

    lane = lax.broadcasted_iota(I32, (QBLOCK, LANES), 1)
    low = lane < HEAD_DIM
    low2 = lax.broadcasted_iota(I32, (2 * QBLOCK, LANES), 1) < HEAD_DIM
    hrow = lax.broadcasted_iota(I32, (rows, 1), 0) // QBLOCK
    head_order = (0, 2, 1, 3)
    sinks = []
    for hk in range(N_KV_HEADS):
        sink = jnp.zeros((rows, 1), F32)
        for g in range(grp):
            sink = jnp.where(hrow == g, sink_ref[hk * grp + head_order[g]], sink)
        sinks.append(sink)

    def qblock(jb, carry):
        r0 = pl.multiple_of(jb * QBLOCK, QBLOCK)
        pos0 = i * tq + r0
        bias_prev = bias_ref[jnp.where(pos0 == 0, 1, 0)]
        bias_next = bias_ref[jnp.where(pos0 + QBLOCK == seq, 3, 2)]
        qb = q_ref[pl.ds(r0, QBLOCK), :]
        for hk in range(N_KV_HEADS):
            q2 = [qb[:, (2 * hk + cc) * LANES:(2 * hk + cc + 1) * LANES] for cc in range(2)]
            qs = jnp.concatenate(
                [jnp.where(low, q, jnp.zeros_like(q)) for q in q2]
                + [jnp.where(low, jnp.zeros_like(q), q) for q in q2], axis=0)
            sink = sinks[hk]
            s = lax.dot_general(qs, kd_ref[hk, pl.ds(r0, band), :], (((1,), (1,)), ((), ())),
                                preferred_element_type=F32)
            s = jnp.concatenate([s[:, :QBLOCK] + jnp.concatenate([bias_prev] * grp, axis=0),
                                 s[:, QBLOCK:2 * QBLOCK],
                                 s[:, 2 * QBLOCK:] + jnp.concatenate([bias_next] * grp, axis=0)],
                                axis=1)
            m = jnp.maximum(jnp.max(s, axis=1, keepdims=True), sink)
            p = jnp.exp((s - m).astype(BF16))
            sink_p = jnp.exp(sink - m)
            half_rows = 2 * QBLOCK
            o_lo = jnp.dot(p[:half_rows], vd_ref[2 * hk, pl.ds(r0, band), :],
                           preferred_element_type=F32)
            o_hi = jnp.dot(p[half_rows:], vd_ref[2 * hk + 1, pl.ds(r0, band), :],
                           preferred_element_type=F32)
            den_lo = jnp.where(low2, pltpu.roll(o_lo, HEAD_DIM, axis=1), o_lo) + sink_p[:half_rows]
            den_hi = jnp.where(low2, o_hi, pltpu.roll(o_hi, HEAD_DIM, axis=1)) + sink_p[half_rows:]
            y_lo = o_lo * (1.0 / den_lo)
            y_hi = o_hi * (1.0 / den_hi)
            for cc in range(2):
                c = 2 * hk + cc
                ya_ref[pl.ds(r0, QBLOCK), c * LANES:(c + 1) * LANES] = jnp.where(
                    low, y_lo[cc * QBLOCK:(cc + 1) * QBLOCK], y_hi[cc * QBLOCK:(cc + 1) * QBLOCK])
        return carry

    for jb in range(tq // QBLOCK):
        qblock(jb, 0)

    own, prev, nxt = edge_ref[0], edgep_ref[0], edgen_ref[0]
    first = jnp.where(i > 0, own[2:3] * cw_ref[0:1, :] * prev[1:2], 0.0)
    last = jnp.where(i < nseq - 1, own[3:4] * cw_ref[2:3, :] * nxt[0:1], 0.0)
    eg = 2 * SUBLANES
    re = lax.broadcasted_iota(I32, (eg, CONV_WIDTH), 0)
    yc = jnp.concatenate(
        [yc_ref[0:eg, :].astype(F32) + jnp.where(re == 0, first, 0.0),
         yc_ref[eg:tq - eg, :].astype(F32),
         yc_ref[tq - eg:tq, :].astype(F32) + jnp.where(re == eg - 1, last, 0.0)],
        axis=0)
    yc = _rms(yc, gc_ref[...]).astype(BF16)
    ya = _rms(ya_ref[...], ga_ref[...]).astype(BF16)
    o_ref[...] = (x_ref[...]
                  + jnp.dot(ya, wo_ref[:ATTN_WIDTH, :], preferred_element_type=F32)
                  + jnp.dot(yc, wo_ref[ATTN_WIDTH:, :], preferred_element_type=F32))
    for u in range(xs_ref.shape[0]):
        _route_tile(u, o_ref, gf_ref, wr_ref, br_ref, xs_ref, s_ref, meta_ref)


def _mix(x2, q, k, v, yc, edge, sink, conv_w, g_attn, g_conv, w_out_b, g_ffn, w_router, b_router,
         batch, seq):
    t, d = x2.shape
    tq = MIX_ROWS
    assert tq == FRONT_ROWS
    nseq = seq // tq
    per = tq // SORT_ROWS
    nt = t // SORT_ROWS
    tile = lambda b, i: (b * nseq + i, 0, 0)
    tprev = lambda b, i: (b * nseq + jnp.maximum(i - 1, 0), 0, 0)
    tnext = lambda b, i: (b * nseq + jnp.minimum(i + 1, nseq - 1), 0, 0)
    kb = tq // QBLOCK
    nkb = seq // QBLOCK

    row = lambda b, i: (b * nseq + i, 0)
    kprev = lambda b, i: (b * nkb + jnp.maximum(i * kb - 1, 0), 0)
    knext = lambda b, i: (b * nkb + jnp.minimum((i + 1) * kb, nkb - 1), 0)
    fixed = lambda b, i: (0, 0)
    edge_block = (1, SUBLANES, CONV_WIDTH)
    qi = lax.broadcasted_iota(I32, (QBLOCK, QBLOCK), 0)
    kj = lax.broadcasted_iota(I32, (QBLOCK, QBLOCK), 1)
    neg = jnp.full((QBLOCK, QBLOCK), NEG_INF, F32)
    band_bias = jnp.stack([jnp.where(kj - QBLOCK - qi >= -WINDOW, 0.0, NEG_INF).astype(F32), neg,
                           jnp.where(kj + QBLOCK - qi <= WINDOW, 0.0, NEG_INF).astype(F32), neg])
    return pl.pallas_call(
        functools.partial(_mix_kernel, seq=seq),
        grid=(batch, nseq),
        in_specs=[
            pl.BlockSpec(memory_space=pltpu.SMEM),
            pl.BlockSpec((tq, d), row),
            pl.BlockSpec((tq, ATTN_WIDTH), row),
            pl.BlockSpec((QBLOCK, KV_WIDTH), kprev),
            pl.BlockSpec((tq, KV_WIDTH), row),
            pl.BlockSpec((QBLOCK, KV_WIDTH), knext),
            pl.BlockSpec((QBLOCK, KV_WIDTH), kprev),
            pl.BlockSpec((tq, KV_WIDTH), row),
            pl.BlockSpec((QBLOCK, KV_WIDTH), knext),
            pl.BlockSpec((tq, CONV_WIDTH), row),
            pl.BlockSpec(edge_block, tile),
            pl.BlockSpec(edge_block, tprev),
            pl.BlockSpec(edge_block, tnext),
            pl.BlockSpec(conv_w.shape, fixed),
            pl.BlockSpec((1, ATTN_WIDTH), fixed),
            pl.BlockSpec((1, CONV_WIDTH), fixed),
            pl.BlockSpec(w_out_b.shape, fixed),
            pl.BlockSpec(band_bias.shape, lambda b, i: (0, 0, 0)),
            pl.BlockSpec((1, d), fixed),
            pl.BlockSpec(w_router.shape, fixed),
            pl.BlockSpec(b_router.shape, fixed),
        ],
        out_specs=[
            pl.BlockSpec((tq, d), row),
            pl.BlockSpec((per, SORT_CAP, XS_W), tile),
            pl.BlockSpec((per, SUBLANES, SORT_ROWS), tile),
            pl.BlockSpec((per, SUBLANES, LANES), tile),
        ],
        out_shape=[
            jax.ShapeDtypeStruct((t, d), F32),
            jax.ShapeDtypeStruct((nt, SORT_CAP, XS_W), U32),
            jax.ShapeDtypeStruct((nt, SUBLANES, SORT_ROWS), F32),
            jax.ShapeDtypeStruct((nt, SUBLANES, LANES), I32),
        ],
        scratch_shapes=[pltpu.VMEM((tq, ATTN_WIDTH), F32),
                        pltpu.VMEM((N_KV_HEADS, tq + 2 * QBLOCK, LANES), BF16),
                        pltpu.VMEM((2 * N_KV_HEADS, tq + 2 * QBLOCK, LANES), BF16)],
        compiler_params=pltpu.CompilerParams(
            dimension_semantics=("parallel", "parallel"), vmem_limit_bytes=VMEM_LIMIT),
        name="mix",
    )(sink, x2, q, k, k, k, v, v, v, yc, edge, edge, edge, conv_w, g_attn, g_conv, w_out_b, band_bias,
      g_ffn, w_router, b_router)


ROUTER_ROWS = N_EXPERTS + SUBLANES


def _route_tile(u, x_ref, g_ref, w_ref, b_ref, xs_ref, s_ref, meta_ref):
    tl = SORT_ROWS
    h = _rms(x_ref[u * tl:(u + 1) * tl, :], g_ref[...])
    hb = h.astype(BF16)

    dn = (((1,), (1,)), ((), ()))
    h2 = (h - hb.astype(F32)).astype(BF16)
    both = lax.dot_general(w_ref[...], hb, dn, preferred_element_type=F32)
    corr = lax.dot_general(w_ref[:ROUTER_ROWS, :], h2, dn, preferred_element_type=F32)
    logits = both[:ROUTER_ROWS] + both[ROUTER_ROWS:] + corr + b_ref[...]
    el = logits[:N_EXPERTS]
    gl = logits[N_EXPERTS:]

    r8 = lax.broadcasted_iota(I32, gl.shape, 0)
    glm = jnp.where(r8 < N_GROUPS, gl, -jnp.inf)
    gmax = jnp.max(glm, axis=0, keepdims=True)
    grp = jnp.min(jnp.where(glm == gmax, r8, SUBLANES), axis=0, keepdims=True)
    p_grp = 1.0 / jnp.sum(jnp.exp(glm - gmax), axis=0, keepdims=True)

    r32 = lax.broadcasted_iota(I32, el.shape, 0)
    e0 = jnp.where((r32 // PER_GROUP) == grp, el, -jnp.inf)
    t1 = jnp.max(e0, axis=0, keepdims=True)
    i1 = jnp.min(jnp.where(e0 == t1, r32, N_EXPERTS), axis=0, keepdims=True)
    a1 = r32 == i1
    e1 = jnp.where(a1, -jnp.inf, e0)
    t2 = jnp.max(e1, axis=0, keepdims=True)
    i2 = jnp.min(jnp.where(e1 == t2, r32, N_EXPERTS), axis=0, keepdims=True)
    a2 = r32 == i2
    ed = jnp.exp(t2 - t1)
    inv = 1.0 / (1.0 + ed)
    g1 = p_grp * inv
    g2 = p_grp * (ed * inv)

    af = (a1 | a2).astype(F32)
    cnt = jnp.sum(af, axis=1, keepdims=True)
    pc = jnp.floor((cnt + (CHUNK - 1)) * (1.0 / CHUNK)) * CHUNK
    er = lax.broadcasted_iota(I32, (N_EXPERTS, LANES), 0)
    ec = lax.broadcasted_iota(I32, (N_EXPERTS, LANES), 1)
    pc_row = jnp.sum(jnp.where(er == ec, pc, 0.0), axis=0, keepdims=True)
    lo_row = jnp.sum(jnp.where(er < ec, pc, 0.0), axis=0, keepdims=True)
    lo_col = jnp.sum(jnp.where(ec < er, pc_row, 0.0), axis=1, keepdims=True)
    total = jnp.sum(pc, axis=0, keepdims=True)

    tr = lax.broadcasted_iota(I32, (tl, tl), 0)
    tc = lax.broadcasted_iota(I32, (tl, tl), 1)
    upper = (tr <= tc).astype(BF16)
    incl = jnp.dot(af.astype(BF16), upper, preferred_element_type=F32)
    slot = lo_col + incl - af
    s1 = jnp.sum(jnp.where(a1, slot, 0.0), axis=0, keepdims=True).astype(I32)
    s2 = jnp.sum(jnp.where(a2, slot, 0.0), axis=0, keepdims=True).astype(I32)

    cap = xs_ref.shape[1]
    srow = lax.broadcasted_iota(I32, (cap, tl), 0)
    p1 = srow == s1
    p2 = srow == s2
    perm = (p1 | p2).astype(BF16)
    half = hb.shape[1] // 2
    lo = jnp.dot(perm, hb[:, :half], preferred_element_type=F32)
    hi = jnp.dot(perm, hb[:, half:], preferred_element_type=F32)
    xs_ref[u] = _pack2(lo, hi)

    sr = lax.broadcasted_iota(I32, (SUBLANES, tl), 0)
    s_ref[u] = jnp.where(sr == 0, s1.astype(F32), jnp.where(sr == 1, s2.astype(F32),
                         jnp.where(sr == 2, g1, jnp.where(sr == 3, g2, 0.0))))
    mr = lax.broadcasted_iota(I32, (SUBLANES, LANES), 0)
    meta = jnp.where(mr == 0, pc_row, jnp.where(mr == 1, lo_row, jnp.where(mr == 2, total, 0.0)))
    meta_ref[u] = (meta * (1.0 / CHUNK)).astype(I32)


def _router_params(w_group, b_group, w_expert, b_expert):
    pad = SUBLANES - N_GROUPS
    w = jnp.concatenate([jnp.swapaxes(w_expert, 1, 2),
                         jnp.pad(jnp.swapaxes(w_group, 1, 2), ((0, 0), (0, pad), (0, 0)))], axis=1)
    w1 = w.astype(BF16)
    w2 = (w - w1.astype(F32)).astype(BF16)
    b = jnp.concatenate([b_expert, jnp.pad(b_group, ((0, 0), (0, pad)))], axis=1)[:, :, None]
    return jnp.concatenate([w1, w2], axis=1), b


def _n_blocks(t):
    nt = t // SORT_ROWS
    max_chunks = (2 * t + (CHUNK - 1) * N_EXPERTS * nt) // CHUNK + (BLOCK_CHUNKS - 1) * N_EXPERTS
    return -(-max_chunks // BLOCK_CHUNKS)


PLAN_LANES = 1024


def _list_len(t):
    return -(-(_n_blocks(t) + 1) * BLOCK_CHUNKS // PLAN_LANES) * PLAN_LANES


def _plan_kernel(nch_ref, ncht_ref, lo8_ref, csrc_ref, cdst_ref, bend_ref):
    nt = nch_ref.shape[0]
    ch = csrc_ref.shape[1]
    nch = nch_ref[:, :N_EXPERTS].astype(F32)
    lo8 = lo8_ref[:, :N_EXPERTS].astype(F32)
    ncht = ncht_ref[...].astype(F32)

    tc = jnp.sum(ncht, axis=1, keepdims=True)
    ptc = jnp.floor((tc + (BLOCK_CHUNKS - 1)) * (1.0 / BLOCK_CHUNKS)) * BLOCK_CHUNKS
    er = lax.broadcasted_iota(I32, (N_EXPERTS, LANES), 0)
    ec = lax.broadcasted_iota(I32, (N_EXPERTS, LANES), 1)
    ptc_row = jnp.sum(jnp.where(er == ec, ptc, 0.0), axis=0, keepdims=True)
    eend = jnp.sum(jnp.where(ec <= er, ptc_row, 0.0), axis=1, keepdims=True)
    estart = eend - ptc
    total = jnp.sum(ptc, axis=0, keepdims=True)

    tr = lax.broadcasted_iota(I32, (nt, nt), 0)
    tcol = lax.broadcasted_iota(I32, (nt, nt), 1)
    lower = (tcol <= tr).astype(BF16)
    cum = jnp.dot(lower, nch.astype(BF16), preferred_element_type=F32)
    cum_hi = jnp.floor(cum * (1.0 / LANES))
    cum_lo = cum - cum_hi * LANES

    k = (pl.program_id(0) * ch + lax.broadcasted_iota(I32, (1, ch), 1)).astype(F32)
    r32 = lax.broadcasted_iota(I32, (N_EXPERTS, ch), 0).astype(F32)

    def expert_of(pos):
        return jnp.minimum(jnp.sum((eend <= pos).astype(F32), axis=0, keepdims=True),
                           N_EXPERTS - 1.0)

    e_k = expert_of(k)
    sel_e = r32 == e_k
    oh_e = sel_e.astype(BF16)
    start_k = jnp.sum(jnp.where(sel_e, estart, 0.0), axis=0, keepdims=True)
    tc_k = jnp.sum(jnp.where(sel_e, tc, 0.0), axis=0, keepdims=True)
    q = k - start_k

    def pick(tbl):
        return jnp.dot(tbl.astype(BF16), oh_e, preferred_element_type=F32)

    cum_e = pick(cum_hi) * LANES + pick(cum_lo)
    nch_e = pick(nch)
    lo8_e = pick(lo8)
    rnt = lax.broadcasted_iota(I32, (nt, ch), 0).astype(F32)
    i_k = jnp.minimum(jnp.sum((cum_e <= q).astype(F32), axis=0, keepdims=True), nt - 1.0)
    sel_i = rnt == i_k
    before = jnp.sum(jnp.where(sel_i, cum_e - nch_e, 0.0), axis=0, keepdims=True)
    lo8_k = jnp.sum(jnp.where(sel_i, lo8_e, 0.0), axis=0, keepdims=True)
    src = i_k * TILE_CHUNKS + lo8_k + (q - before)
    pad = (q >= tc_k) | (k >= total)
    csrc_ref[...] = jnp.where(pad, TILE_CHUNKS - 1.0, src).astype(I32)
    cdst_ref[...] = jnp.where(pad, nt * TILE_CHUNKS + k, src).astype(I32)

    eend_row = jnp.sum(jnp.where(er <= ec, ptc, 0.0), axis=0, keepdims=True)
    bend_ref[...] = (eend_row * (1.0 / BLOCK_CHUNKS)).astype(I32)


def _plan(meta, t):
    nt = t // SORT_ROWS
    nl = _list_len(t)
    nch = meta[:, 0, :]
    lo8 = meta[:, 1, :]
    ncht = nch[:, :N_EXPERTS].T
    fixed = lambda c: (0, 0)
    csrc, cdst, bend = pl.pallas_call(
        _plan_kernel,
        grid=(nl // PLAN_LANES,),
        in_specs=[
            pl.BlockSpec(nch.shape, fixed),
            pl.BlockSpec(ncht.shape, fixed),
            pl.BlockSpec(lo8.shape, fixed),
        ],
        out_specs=[
            pl.BlockSpec((1, PLAN_LANES), lambda c: (0, c)),
            pl.BlockSpec((1, PLAN_LANES), lambda c: (0, c)),
            pl.BlockSpec((1, LANES), fixed),
        ],
        out_shape=[
            jax.ShapeDtypeStruct((1, nl), I32),
            jax.ShapeDtypeStruct((1, nl), I32),
            jax.ShapeDtypeStruct((1, LANES), I32),
        ],
        compiler_params=pltpu.CompilerParams(dimension_semantics=("arbitrary",)),
        name="plan",
    )(nch, ncht, lo8)
    return bend.reshape(-1), csrc.reshape(-1), cdst.reshape(-1)


def _expert_kernel(bend_ref, csrc_ref, cdst_ref,
                   xs_hbm, wg_ref, wu_ref, wd_ref, ys_hbm,
                   xbuf, ybuf, wgb, wub, wdb, sem_in, sem_out):
    e = pl.program_id(0)
    n_used = bend_ref[N_EXPERTS - 1]
    b_lo = jnp.where(e == 0, 0, bend_ref[jnp.maximum(e - 1, 0)])
    b_hi = bend_ref[e]

    def gather(blk, sl):
        return [pltpu.make_async_copy(xs_hbm.at[csrc_ref[blk * BLOCK_CHUNKS + j]],
                                      xbuf.at[sl, j], sem_in.at[sl])
                for j in range(BLOCK_CHUNKS)]

    def scatter(blk, sl):
        return [pltpu.make_async_copy(ybuf.at[sl, j],
                                      ys_hbm.at[cdst_ref[blk * BLOCK_CHUNKS + j]], sem_out.at[sl])
                for j in range(BLOCK_CHUNKS)]

    @pl.when(e == 0)
    def _():
        for c in gather(0, 0):
            c.start()

    @pl.when(b_hi > b_lo)
    def _():
        wgb[...] = wg_ref[0, 0].astype(BF16)
        wub[...] = wu_ref[0, 0].astype(BF16)
        wdb[...] = wd_ref[0, 0].astype(BF16)

    def block(b, carry):
        slot = lax.rem(b, 2)
        for c in gather(b, slot):
            c.wait()
        for c in gather(b + 1, 1 - slot):
            c.start()
        xw = xbuf[slot].reshape(BLOCK_ROWS, XS_W)
        lo = _unpack2(xw, 0).astype(BF16)
        hi = _unpack2(xw, 1).astype(BF16)
        half = PACK_W
        hg = (jnp.dot(lo, wgb[:half, :], preferred_element_type=F32)
              + jnp.dot(hi, wgb[half:, :], preferred_element_type=F32))
        hu = (jnp.dot(lo, wub[:half, :], preferred_element_type=F32)
              + jnp.dot(hi, wub[half:, :], preferred_element_type=F32))
        act = (hg * (1.0 / (1.0 + jnp.exp(-hg))) * hu).astype(BF16)
        y = jnp.dot(act, wdb[...], preferred_element_type=F32)
        ybuf[slot] = _pack2(y[:, :half], y[:, half:]).reshape(BLOCK_CHUNKS, CHUNK, PACK_W)
        for c in scatter(b, slot):
            c.start(priority=1)

        @pl.when(b >= 1)
        def _():
            for c in scatter(b - 1, 1 - slot):
                c.wait()

        return carry

    lax.fori_loop(b_lo, b_hi, block, 0)

    @pl.when(e == N_EXPERTS - 1)
    def _():
        last = n_used - 1
        for c in scatter(last, lax.rem(last, 2)):
            c.wait()
        for c in gather(n_used, lax.rem(n_used, 2)):
            c.wait()


def _experts(xs, bend, csrc, cdst, w_gate, w_up, w_down, layer, t):
    nt = t // SORT_ROWS
    d = w_gate.shape[2]
    xs3 = xs.reshape(nt * TILE_CHUNKS, CHUNK, XS_W)
    n_out = nt * TILE_CHUNKS + _list_len(t)
    wmap = lambda e, be, cs, cd: (layer, e, 0, 0)
    grid_spec = pltpu.PrefetchScalarGridSpec(
        num_scalar_prefetch=3,
        grid=(N_EXPERTS,),
        in_specs=[
            pl.BlockSpec(memory_space=pl.ANY),
            pl.BlockSpec((1, 1, d, D_EXPERT), wmap),
            pl.BlockSpec((1, 1, d, D_EXPERT), wmap),
            pl.BlockSpec((1, 1, D_EXPERT, d), wmap),
        ],
        out_specs=pl.BlockSpec(memory_space=pl.ANY),
        scratch_shapes=[
            pltpu.VMEM((2, BLOCK_CHUNKS, CHUNK, XS_W), U32),
            pltpu.VMEM((2, BLOCK_CHUNKS, CHUNK, PACK_W), U32),
            pltpu.VMEM((d, D_EXPERT), BF16),
            pltpu.VMEM((d, D_EXPERT), BF16),
            pltpu.VMEM((D_EXPERT, d), BF16),
            pltpu.SemaphoreType.DMA((2,)),
            pltpu.SemaphoreType.DMA((2,)),
        ],
    )
    return pl.pallas_call(
        _expert_kernel,
        grid_spec=grid_spec,
        out_shape=jax.ShapeDtypeStruct((n_out, CHUNK, PACK_W), U32),
        compiler_params=pltpu.CompilerParams(
            dimension_semantics=("arbitrary",), vmem_limit_bytes=VMEM_LIMIT),
        name="experts",
    )(bend, csrc, cdst, xs3, w_gate, w_up, w_down)


def _combine_kernel(x_ref, ys_ref, s_ref, meta_ref, g_ref, o_ref):
    _combine_tiles(x_ref, ys_ref, s_ref, meta_ref, o_ref, g_ref)


def _combine_tiles(x_ref, ys_ref, s_ref, meta_ref, o_ref, g_ref):
    tl = SORT_ROWS
    cap = SORT_CAP
    tr = lax.broadcasted_iota(I32, (tl, tl), 0)
    tc = lax.broadcasted_iota(I32, (tl, tl), 1)
    eye = tr == tc
    col = lax.broadcasted_iota(I32, (tl, cap), 1).astype(F32)
    yrow = lax.broadcasted_iota(I32, (cap, PACK_W), 0)
    for u in range(s_ref.shape[0]):
        used = meta_ref[u, 2:3, 0:1] * CHUNK
        ys = ys_ref[u * cap:(u + 1) * cap, :]
        live = yrow < used
        ylo = jnp.where(live, _unpack2(ys, 0), 0.0).astype(BF16)
        yhi = jnp.where(live, _unpack2(ys, 1), 0.0).astype(BF16)
        s = s_ref[u]
        s1, s2, g1, g2 = [jnp.sum(jnp.where(eye, s[r:r + 1, :], 0.0), axis=1, keepdims=True)
                          for r in range(4)]
        perm_t = (jnp.where(col == s1, g1, 0.0) + jnp.where(col == s2, g2, 0.0)).astype(BF16)
        half = PACK_W
        rows = slice(u * tl, (u + 1) * tl)
        out = jnp.concatenate(
            [x_ref[rows, :half] + jnp.dot(perm_t, ylo, preferred_element_type=F32),
             x_ref[rows, half:] + jnp.dot(perm_t, yhi, preferred_element_type=F32)], axis=1)
        if g_ref is not None:
            out = _rms(out, g_ref[...])
        o_ref[rows, :] = out


def _combine_norm(x2, ys, s12, meta, g_final):
    t, d = x2.shape
    tl = SORT_ROWS
    nt = t // tl
    ys2 = ys.reshape(ys.shape[0] * CHUNK, PACK_W)
    per = TILES_PER_STEP
    return pl.pallas_call(
        _combine_kernel,
        grid=(nt // per,),
        in_specs=[
            pl.BlockSpec((per * tl, d), lambda i: (i, 0)),
            pl.BlockSpec((per * SORT_CAP, PACK_W), lambda i: (i, 0)),
            pl.BlockSpec((per, SUBLANES, tl), lambda i: (i, 0, 0)),
            pl.BlockSpec((per, SUBLANES, LANES), lambda i: (i, 0, 0)),
            pl.BlockSpec((1, d), lambda i: (0, 0)),
        ],
        out_specs=pl.BlockSpec((per * tl, d), lambda i: (i, 0)),
        out_shape=jax.ShapeDtypeStruct((t, d), F32),
        compiler_params=pltpu.CompilerParams(
            dimension_semantics=("parallel",), vmem_limit_bytes=VMEM_LIMIT),
        name="combine",
    )(x2, ys2, s12, meta, g_final)


def _rope_tables(seq):
    pos = jnp.arange(seq, dtype=F32)
    inv_freq = ROPE_THETA ** (-jnp.arange(0, ROPE_DIM, 2, dtype=F32) / ROPE_DIM)
    ang = pos[:, None] * inv_freq[None, :]
    cos, sin = jnp.cos(ang), jnp.sin(ang)
    half = ROPE_DIM // 2
    ones = jnp.ones((seq, HEAD_DIM - ROPE_DIM), F32)
    zeros_h = jnp.zeros((seq, half), F32)
    zeros_r = jnp.zeros((seq, HEAD_DIM - ROPE_DIM), F32)
    cos_h = jnp.concatenate([cos, cos, ones], axis=1)
    sp_h = jnp.concatenate([zeros_h, sin, zeros_r], axis=1)
    sm_h = jnp.concatenate([-sin, zeros_h, zeros_r], axis=1)
    rep = LANES // HEAD_DIM
    return (jnp.tile(cos_h, (1, rep)), jnp.tile(sp_h, (1, rep)), jnp.tile(sm_h, (1, rep)))


def kernel(x, norm_mix, w_in, attn_sink, conv_w, norm_attn_out, norm_conv_out, w_out, norm_ffn,
           w_router_group, b_router_group, w_router_expert, b_router_expert,
           w_expert_gate, w_expert_up, w_expert_down, norm_final):
    batch, seq, d = x.shape
    depth = w_in.shape[0]
    t = batch * seq
    assert seq % MIX_ROWS == 0 and seq % FRONT_ROWS == 0 and t % SORT_ROWS == 0
    cos_t, sp_t, sm_t = _rope_tables(seq)
    x2 = x.reshape(t, d)
    w_in_b = w_in.astype(BF16)
    w_out_b = w_out.astype(BF16)
    w_router, b_router = _router_params(w_router_group, b_router_group,
                                        w_router_expert, b_router_expert)
    g_final = norm_final.reshape(1, d)

    ys = s12 = meta = None
    for l in range(depth):
        g_mix = norm_mix[l].reshape(1, d)
        if l == 0:
            q, k, v, yc, edge = _front(x2, g_mix, w_in_b[l], cos_t, sp_t, sm_t, conv_w[l], seq)
        else:
            x2, q, k, v, yc, edge = _combine_front(x2, ys, s12, meta, g_mix, w_in_b[l],
                                                   cos_t, sp_t, sm_t, conv_w[l], seq)
        x2, xs, s12, meta = _mix(x2, q, k, v, yc, edge, attn_sink[l], conv_w[l],
                                 norm_attn_out[l].reshape(1, -1), norm_conv_out[l].reshape(1, -1),
                                 w_out_b[l], norm_ffn[l].reshape(1, d), w_router[l], b_router[l],
                                 batch, seq)
        bend, csrc, cdst = _plan(meta, t)
        ys = _experts(xs, bend, csrc, cdst,
                      w_expert_gate, w_expert_up, w_expert_down, l, t)
    return _combine_norm(x2, ys, s12, meta, g_final).reshape(batch, seq, d)
```
